```python
import math, functools
import jax, jax.numpy as jnp
from jax import lax
import numpy as np

D_MODEL = 1024
BATCH = 4
SEQ = 4096
DEPTH = 2
DEC_BATCH = 32
DEC_SEQ = 1
PAST_LEN = 16384
PAGE_SIZE = 128

N_HEADS = 16
HEAD_DIM = 64
N_KV_HEADS = 4
Q_PER_KV = N_HEADS // N_KV_HEADS
CMP_BLOCK = 64
N_SEL = 16
WINDOW = 512
SEL_QCHUNK = 32
WIN_QBLOCK = 128
FORCE_LOCAL = 1e4
FORCE_INIT = 5e3
D_RNN = D_MODEL
RG_BLOCKS = 16
RG_C = 8.0
RG_CONV = 4
D_CONV = D_MODEL
CONF_WIDTH = 31
D_FF = -(-8 * D_MODEL // (3 * 256)) * 256
ROPE_THETA = 10000.0
EPS = 1e-6
NEG_INF = -1e30
IN_SIZES = (D_RNN, D_RNN, 2 * D_CONV, N_HEADS * HEAD_DIM, 6 * N_KV_HEADS * HEAD_DIM, 3 * N_HEADS, 3 * D_MODEL)
N_IN = sum(IN_SIZES)

kernel_name = 'hybrid_rglru_conformer_nsa_step'


def in_split_points():
    return [int(v) for v in np.cumsum(IN_SIZES)[:-1]]


def window_buffer_len():
    return min(WINDOW, PAST_LEN)


def rmsnorm(x, g):
    xf = x.astype(jnp.float32)
    y = xf * lax.rsqrt(jnp.mean(xf * xf, axis=-1, keepdims=True) + EPS)
    return (y * g.astype(jnp.float32)).astype(x.dtype)


def layernorm(x, g, b):
    xf = x.astype(jnp.float32)
    mu = jnp.mean(xf, axis=-1, keepdims=True)
    var = jnp.mean(jnp.square(xf - mu), axis=-1, keepdims=True)
    return ((xf - mu) * lax.rsqrt(var + EPS) * g.astype(jnp.float32) + b.astype(jnp.float32)).astype(x.dtype)


def rope(x, pos):
    half = x.shape[-1] // 2
    inv = ROPE_THETA ** (-jnp.arange(half, dtype=jnp.float32) / half)
    ang = pos.astype(jnp.float32)[:, None] * inv[None, :]
    cos = jnp.cos(ang)[None, :, None, :]
    sin = jnp.sin(ang)[None, :, None, :]
    xf = x.astype(jnp.float32)
    x1, x2 = xf[..., :half], xf[..., half:]
    return jnp.concatenate([x1 * cos - x2 * sin, x2 * cos + x1 * sin], axis=-1).astype(x.dtype)


def causal_depthwise_conv(x, buf, w, b):
    xc = jnp.concatenate([buf.astype(x.dtype), x], axis=1)
    y = lax.conv_general_dilated(xc, w[:, None, :].astype(x.dtype), window_strides=(1,), padding='VALID',
                                 dimension_numbers=('NWC', 'WIO', 'NWC'), feature_group_count=x.shape[-1])
    return y + b.astype(x.dtype), xc[:, xc.shape[1] - (w.shape[0] - 1):]


def rglru(x, h0, w_a, b_a, w_i, b_i, lam):
    B, T, D = x.shape
    xb = x.reshape(B, T, RG_BLOCKS, D // RG_BLOCKS)
    r = jax.nn.sigmoid((jnp.einsum('btnc,ncd->btnd', xb, w_a).reshape(B, T, D) + b_a).astype(jnp.float32))
    i = jax.nn.sigmoid((jnp.einsum('btnc,ncd->btnd', xb, w_i).reshape(B, T, D) + b_i).astype(jnp.float32))
    log_a = -RG_C * r * jax.nn.softplus(-lam.astype(jnp.float32))
    a = jnp.exp(log_a)
    bx = jnp.sqrt(-jnp.expm1(2.0 * log_a)) * (i * x.astype(jnp.float32))

    def step(h, ab):
        h = ab[0] * h + ab[1]
        return h, h

    h_last, hs = lax.scan(step, h0.astype(jnp.float32), (a.swapaxes(0, 1), bx.swapaxes(0, 1)))
    return hs.swapaxes(0, 1).astype(x.dtype), h_last


def pad_to_block(x):
    pad = (-x.shape[1]) % CMP_BLOCK
    if pad == 0:
        return x
    return jnp.pad(x, ((0, 0), (0, pad)) + ((0, 0),) * (x.ndim - 2))


def last_rows(x, n):
    T = x.shape[1]
    if T < n:
        x = jnp.pad(x, ((0, 0), (n - T, 0)) + ((0, 0),) * (x.ndim - 2))
    return x[:, x.shape[1] - n:]


def band_attention(q, k, v, q_pos, k_pos):
    s = jnp.einsum('bqgrd,bkgd->bqgrk', q, k).astype(jnp.float32) * (q.shape[-1] ** -0.5)
    diff = q_pos[:, None] - k_pos[None, :]
    m = (diff >= 0) & (diff <= WINDOW) & (k_pos[None, :] >= 0)
    p = jax.nn.softmax(jnp.where(m[None, :, None, None, :], s, NEG_INF), axis=-1)
    return jnp.einsum('bqgrk,bkgd->bqgrd', p.astype(v.dtype), v)


def window_banded(q, k, v):
    B, S = q.shape[0], q.shape[1]
    qb = math.gcd(S, WIN_QBLOCK)
    pad = ((0, 0), (WINDOW, 0), (0, 0), (0, 0))
    kp, vp = jnp.pad(k, pad), jnp.pad(v, pad)
    span = WINDOW + qb

    def block(i):
        start = i * qb
        q_i = lax.dynamic_slice_in_dim(q, start, qb, axis=1)
        k_i = lax.dynamic_slice_in_dim(kp, start, span, axis=1)
        v_i = lax.dynamic_slice_in_dim(vp, start, span, axis=1)
        q_pos = start + jnp.arange(qb, dtype=jnp.int32)
        k_pos = start - WINDOW + jnp.arange(span, dtype=jnp.int32)
        return band_attention(q_i, k_i, v_i, q_pos, k_pos)

    o = lax.map(block, jnp.arange(S // qb, dtype=jnp.int32))
    return o.swapaxes(0, 1).reshape(q.shape)


def compressed_and_selected(q, kc, vc, ks, vs, q_pos, w_ck, w_cv):
    B, Q, G, R, Dh = q.shape
    nb = kc.shape[1] // CMP_BLOCK
    scale = Dh ** -0.5
    ck = jnp.einsum('bnjgd,jde->bnge', kc.reshape(B, nb, CMP_BLOCK, G, Dh), w_ck)
    cv = jnp.einsum('bnjgd,jde->bnge', vc.reshape(B, nb, CMP_BLOCK, G, Dh), w_cv)
    s = jnp.einsum('bqgrd,bngd->bqgrn', q, ck).astype(jnp.float32) * scale
    blk = jnp.arange(nb, dtype=jnp.int32)
    complete = ((blk + 1) * CMP_BLOCK - 1)[None, :] <= q_pos[:, None]
    cmask = complete[None, :, None, None, :]
    p = jax.nn.softmax(jnp.where(cmask, s, NEG_INF), axis=-1) * cmask
    o_cmp = jnp.einsum('bqgrn,bngd->bqgrd', p.astype(cv.dtype), cv)
    imp = jnp.sum(p, axis=3)
    is_cur = (blk[None, :] == (q_pos // CMP_BLOCK)[:, None])[None, :, None, :]
    avail = ((blk * CMP_BLOCK)[None, :] <= q_pos[:, None])[None, :, None, :]
    imp = jnp.where(is_cur, FORCE_LOCAL, jnp.where((blk == 0)[None, None, None, :], FORCE_INIT,
                                                  jnp.where(avail, imp, -1.0)))
    ksel = min(N_SEL, nb)
    _, idx = lax.top_k(imp, ksel)
    kb = ks.reshape(B, nb, CMP_BLOCK, G, Dh)
    vb = vs.reshape(B, nb, CMP_BLOCK, G, Dh)
    bi = jnp.arange(B)[:, None, None, None]
    gi = jnp.arange(G)[None, None, :, None]
    qc = math.gcd(Q, SEL_QCHUNK)
    nc = Q // qc

    def chunk(args):
        q_c, idx_c, pos_c = args
        kg = kb[bi, idx_c, :, gi, :]
        vg = vb[bi, idx_c, :, gi, :]
        sc = jnp.einsum('bqgrd,bqgkjd->bqgrkj', q_c, kg).astype(jnp.float32) * scale
        tok = idx_c[..., None] * CMP_BLOCK + jnp.arange(CMP_BLOCK, dtype=jnp.int32)
        m = (tok <= pos_c[None, :, None, None, None])[:, :, :, None]
        sc = jnp.where(m, sc, NEG_INF)
        ps = jax.nn.softmax(sc.reshape(B, qc, G, R, ksel * CMP_BLOCK), axis=-1).reshape(sc.shape)
        return jnp.einsum('bqgrkj,bqgkjd->bqgrd', ps.astype(vg.dtype), vg)

    xs = (q.reshape(B, nc, qc, G, R, Dh).swapaxes(0, 1),
          idx.reshape(B, nc, qc, G, ksel).swapaxes(0, 1),
          q_pos.reshape(nc, qc))
    o_slc = lax.map(chunk, xs).swapaxes(0, 1).reshape(B, Q, G, R, Dh)
    return o_cmp, o_slc


def nsa_prompt(q, kv_c, kv_s, kv_w, w_ck, w_cv):
    S = q.shape[1]
    pos = jnp.arange(S, dtype=jnp.int32)
    kc, ks = pad_to_block(kv_c), pad_to_block(kv_s)
    o_c, o_s = compressed_and_selected(q, kc[:, :, 0], kc[:, :, 1], ks[:, :, 0], ks[:, :, 1], pos, w_ck, w_cv)
    o_w = window_banded(q, kv_w[:, :, 0], kv_w[:, :, 1])
    return o_c, o_s, o_w, kv_c, kv_s, last_rows(kv_w, window_buffer_len())


def nsa_sample(q, kv_c, kv_s, kv_w, layer, cache_c, cache_s, cache_w, page_table, w_ck, w_cv):
    DB, DS = q.shape[0], q.shape[1]
    pos = PAST_LEN + jnp.arange(DS, dtype=jnp.int32)

    def full(cache, new):
        past = cache[layer, page_table].reshape(DB, -1, 2, N_KV_HEADS, HEAD_DIM)
        return pad_to_block(jnp.concatenate([past, new.astype(past.dtype)], axis=1))

    kc, ks = full(cache_c, kv_c), full(cache_s, kv_s)
    o_c, o_s = compressed_and_selected(q, kc[:, :, 0], kc[:, :, 1], ks[:, :, 0], ks[:, :, 1], pos, w_ck, w_cv)
    nbuf = cache_w.shape[2]
    wf = jnp.concatenate([cache_w[layer], kv_w.astype(cache_w.dtype)], axis=1)
    k_pos = PAST_LEN - nbuf + jnp.arange(nbuf + DS, dtype=jnp.int32)
    o_w = band_attention(q, wf[:, :, 0], wf[:, :, 1], pos, k_pos)
    return o_c, o_s, o_w, kv_c, kv_s, wf[:, wf.shape[1] - nbuf:]


def trunk_layer(x, pos, rg_buf, rg_h0, cf_buf, nsa_fn, p):
    B, T = x.shape[0], x.shape[1]
    hn = rmsnorm(x, p['norm1_g'])
    z = hn @ p['w_in']
    zr, zg, zc, zq, zkv, zng, zmg = jnp.split(z, in_split_points(), axis=-1)
    xr, rg_buf_new = causal_depthwise_conv(zr, rg_buf, p['rg_conv_w'], p['rg_conv_b'])
    hs, rg_h_new = rglru(xr, rg_h0, p['rg_wa'], p['rg_ba'], p['rg_wi'], p['rg_bi'], p['rg_lambda'])
    y_a = jax.nn.gelu(zg) * hs
    za, zb = jnp.split(zc, 2, axis=-1)
    u, cf_buf_new = causal_depthwise_conv(za * jax.nn.sigmoid(zb), cf_buf, p['cf_dw_w'], p['cf_dw_b'])
    y_b = jax.nn.silu(layernorm(u, p['cf_ln_g'], p['cf_ln_b']))
    q = rope(zq.reshape(B, T, N_HEADS, HEAD_DIM), pos).reshape(B, T, N_KV_HEADS, Q_PER_KV, HEAD_DIM)
    kv = zkv.reshape(B, T, 3, 2, N_KV_HEADS, HEAD_DIM)
    k = rope(kv[:, :, :, 0].reshape(B, T, 3 * N_KV_HEADS, HEAD_DIM), pos).reshape(B, T, 3, N_KV_HEADS, HEAD_DIM)
    kv = jnp.stack([k, kv[:, :, :, 1]], axis=3)
    o_cmp, o_slc, o_win, kv_c_new, kv_s_new, win_new = nsa_fn(q, kv[:, :, 0], kv[:, :, 1], kv[:, :, 2])
    g = jax.nn.sigmoid(zng.reshape(B, T, N_KV_HEADS, Q_PER_KV, 3, 1))
    y_c = (g[..., 0, :] * o_cmp + g[..., 1, :] * o_slc + g[..., 2, :] * o_win).reshape(B, T, N_HEADS * HEAD_DIM)
    mg = jax.nn.sigmoid(zmg.reshape(B, T, 3, D_MODEL))
    m = (mg[:, :, 0] * (y_a @ p['w_br_rnn']) + mg[:, :, 1] * (y_b @ p['w_br_conv'])
         + mg[:, :, 2] * (y_c @ p['w_br_attn']))
    x = x + m @ p['w_out']
    f_g, f_u = jnp.split(rmsnorm(x, p['norm2_g']) @ p['w_ffn_gu'], 2, axis=-1)
    x = x + (jax.nn.silu(f_g) * f_u) @ p['w_ffn_down']
    return x, (kv_c_new, kv_s_new, win_new, rg_buf_new, rg_h_new, cf_buf_new)


def setup_inputs(seed: int = 0) -> dict:
    key = jax.random.key(seed)
    keys = iter(jax.random.split(key, 48))

    def nrm(shape, scale):
        return jax.random.normal(next(keys), shape, jnp.float32) * scale

    n_pages = PAST_LEN // PAGE_SIZE
    n_used = DEC_BATCH * n_pages
    n_phys = n_used + max(1, n_used // 4)
    w_buf = min(WINDOW, PAST_LEN)
    bs = D_RNN // RG_BLOCKS
    kvdims = (2, N_KV_HEADS, HEAD_DIM)
    u = jax.random.uniform(next(keys), (DEPTH, D_RNN), jnp.float32, 0.9, 0.999)
    base = u ** (1.0 / RG_C)
    rg_lambda = jnp.log(base) - jnp.log1p(-base)
    page_table = jax.random.permutation(next(keys), n_phys)[:n_used].reshape(DEC_BATCH, n_pages).astype(jnp.int32)
    return {
        'x_prompt': nrm((BATCH, SEQ, D_MODEL), 1.0),
        'x_sample': nrm((DEC_BATCH, DEC_SEQ, D_MODEL), 1.0),
        'cache_cmp_kv': nrm((DEPTH, n_phys, PAGE_SIZE) + kvdims, 1.0),
        'cache_slc_kv': nrm((DEPTH, n_phys, PAGE_SIZE) + kvdims, 1.0),
        'cache_win_kv': nrm((DEPTH, DEC_BATCH, w_buf) + kvdims, 1.0),
        'state_rglru_conv': nrm((DEPTH, DEC_BATCH, RG_CONV - 1, D_RNN), 1.0),
        'state_rglru_h': nrm((DEPTH, DEC_BATCH, D_RNN), 0.5),
        'state_conf_conv': nrm((DEPTH, DEC_BATCH, CONF_WIDTH - 1, D_CONV), 1.0),
        'page_table': page_table,
        'norm1_g': 1.0 + nrm((DEPTH, D_MODEL), 0.05),
        'w_in': nrm((DEPTH, D_MODEL, N_IN), D_MODEL ** -0.5),
        'rg_conv_w': nrm((DEPTH, RG_CONV, D_RNN), RG_CONV ** -0.5),
        'rg_conv_b': nrm((DEPTH, D_RNN), 0.02),
        'rg_wa': nrm((DEPTH, RG_BLOCKS, bs, bs), bs ** -0.5),
        'rg_ba': nrm((DEPTH, D_RNN), 0.1),
        'rg_wi': nrm((DEPTH, RG_BLOCKS, bs, bs), bs ** -0.5),
        'rg_bi': nrm((DEPTH, D_RNN), 0.1),
        'rg_lambda': rg_lambda,
        'cf_dw_w': nrm((DEPTH, CONF_WIDTH, D_CONV), CONF_WIDTH ** -0.5),
        'cf_dw_b': nrm((DEPTH, D_CONV), 0.02),
        'cf_ln_g': 1.0 + nrm((DEPTH, D_CONV), 0.05),
        'cf_ln_b': nrm((DEPTH, D_CONV), 0.02),
        'nsa_w_ck': nrm((DEPTH, CMP_BLOCK, HEAD_DIM, HEAD_DIM), (CMP_BLOCK * HEAD_DIM) ** -0.5),
        'nsa_w_cv': nrm((DEPTH, CMP_BLOCK, HEAD_DIM, HEAD_DIM), (CMP_BLOCK * HEAD_DIM) ** -0.5),
        'w_br_rnn': nrm((DEPTH, D_RNN, D_MODEL), D_RNN ** -0.5),
        'w_br_conv': nrm((DEPTH, D_CONV, D_MODEL), D_CONV ** -0.5),
        'w_br_attn': nrm((DEPTH, N_HEADS * HEAD_DIM, D_MODEL), (N_HEADS * HEAD_DIM) ** -0.5),
        'w_out': nrm((DEPTH, D_MODEL, D_MODEL), D_MODEL ** -0.5),
        'norm2_g': 1.0 + nrm((DEPTH, D_MODEL), 0.05),
        'w_ffn_gu': nrm((DEPTH, D_MODEL, 2 * D_FF), D_MODEL ** -0.5),
        'w_ffn_down': nrm((DEPTH, D_FF, D_MODEL), D_FF ** -0.5),
        'final_norm_g': 1.0 + nrm((D_MODEL,), 0.05),
    }


def reference(x_prompt, x_sample, cache_cmp_kv, cache_slc_kv, cache_win_kv, state_rglru_conv, state_rglru_h,
              state_conf_conv, page_table, norm1_g, w_in, rg_conv_w, rg_conv_b, rg_wa, rg_ba, rg_wi, rg_bi,
              rg_lambda, cf_dw_w, cf_dw_b, cf_ln_g, cf_ln_b, nsa_w_ck, nsa_w_cv, w_br_rnn, w_br_conv, w_br_attn,
              w_out, norm2_g, w_ffn_gu, w_ffn_down, final_norm_g):
    B, S = x_prompt.shape[0], x_prompt.shape[1]
    pos_p = jnp.arange(S, dtype=jnp.int32)
    pos_s = PAST_LEN + jnp.arange(x_sample.shape[1], dtype=jnp.int32)
    xp, xs = x_prompt, x_sample
    st_p, st_s = [], []
    for l in range(DEPTH):
        lp = {'norm1_g': norm1_g[l], 'w_in': w_in[l], 'rg_conv_w': rg_conv_w[l], 'rg_conv_b': rg_conv_b[l],
              'rg_wa': rg_wa[l], 'rg_ba': rg_ba[l], 'rg_wi': rg_wi[l], 'rg_bi': rg_bi[l], 'rg_lambda': rg_lambda[l],
              'cf_dw_w': cf_dw_w[l], 'cf_dw_b': cf_dw_b[l], 'cf_ln_g': cf_ln_g[l], 'cf_ln_b': cf_ln_b[l],
              'w_br_rnn': w_br_rnn[l], 'w_br_conv': w_br_conv[l], 'w_br_attn': w_br_attn[l], 'w_out': w_out[l],
              'norm2_g': norm2_g[l], 'w_ffn_gu': w_ffn_gu[l], 'w_ffn_down': w_ffn_down[l]}
        nsa_p = functools.partial(nsa_prompt, w_ck=nsa_w_ck[l], w_cv=nsa_w_cv[l])
        nsa_s = functools.partial(nsa_sample, layer=l, cache_c=cache_cmp_kv, cache_s=cache_slc_kv,
                                  cache_w=cache_win_kv, page_table=page_table, w_ck=nsa_w_ck[l], w_cv=nsa_w_cv[l])
        zero_rg = jnp.zeros((B, RG_CONV - 1, D_RNN), xp.dtype)
        zero_h = jnp.zeros((B, D_RNN), jnp.float32)
        zero_cf = jnp.zeros((B, CONF_WIDTH - 1, D_CONV), xp.dtype)
        xp, sp = trunk_layer(xp, pos_p, zero_rg, zero_h, zero_cf, nsa_p, lp)
        xs, ss = trunk_layer(xs, pos_s, state_rglru_conv[l], state_rglru_h[l], state_conf_conv[l], nsa_s, lp)
        st_p.append(sp)
        st_s.append(ss)
    y_prompt = rmsnorm(xp, final_norm_g)
    y_sample = rmsnorm(xs, final_norm_g)

    def stk(sts, j):
        return jnp.stack([st[j] for st in sts], axis=0)

    return (y_prompt, y_sample,
            stk(st_p, 0), stk(st_p, 1), stk(st_p, 2), stk(st_p, 3), stk(st_p, 4), stk(st_p, 5),
            stk(st_s, 0), stk(st_s, 1), stk(st_s, 2), stk(st_s, 3), stk(st_s, 4), stk(st_s, 5))
```

```python
import functools
import math

import numpy as np
import jax
import jax.numpy as jnp
from jax import lax
from jax.experimental import pallas as pl
from jax.experimental.pallas import tpu as pltpu

F32 = jnp.float32
BF16 = jnp.bfloat16

CMP_BLOCK = 64
N_SEL = 16
WINDOW = 512
FORCE_LOCAL = 1e4
FORCE_INIT = 5e3
RG_C = 8.0
ROPE_THETA = 10000.0
EPS = 1e-6
NEG_INF = -1e30

HEAD_DIM = 64
N_KV_HEADS = 4
Q_PER_KV = 4
LANES = 128
VMEM_LIMIT = 56 * 1024 * 1024

D_MODEL = 1024
COL_ZR, COL_ZG, COL_ZA, COL_ZB, COL_ZQ = 0, 1024, 2048, 3072, 4096
COL_KVC, COL_KVS, COL_KVW = 5120, 5632, 6144
COL_NG = 6656
COL_MG = 7168
N_PAD = COL_MG + 3 * D_MODEL
KV_W = 2 * N_KV_HEADS * HEAD_DIM


def _cparams(sem):
    return pltpu.CompilerParams(dimension_semantics=sem, vmem_limit_bytes=VMEM_LIMIT)


def _dot(a, b):
    return jnp.dot(a, b, preferred_element_type=F32)


def _dot_nt(a, b):
    return lax.dot_general(a, b, (((1,), (1,)), ((), ())), preferred_element_type=F32)


def _norm_mm_kernel(x_ref, g_ref, w_ref, o_ref, hn_sc):
    @pl.when(pl.program_id(1) == 0)
    def _():
        x = x_ref[...]
        ms = jnp.mean(x * x, axis=-1, keepdims=True)
        hn_sc[...] = (x * lax.rsqrt(ms + EPS) * g_ref[...]).astype(BF16)

    o_ref[...] = _dot(hn_sc[...], w_ref[...])


def norm_matmul(x, gain, w, tm, tn):
    T, D = x.shape
    N = w.shape[1]
    return pl.pallas_call(
        _norm_mm_kernel,
        grid=(T // tm, N // tn),
        in_specs=[pl.BlockSpec((tm, D), lambda i, j: (i, 0)),
                  pl.BlockSpec((1, D), lambda i, j: (0, 0)),
                  pl.BlockSpec((D, tn), lambda i, j: (0, j))],
        out_specs=pl.BlockSpec((tm, tn), lambda i, j: (i, j)),
        out_shape=jax.ShapeDtypeStruct((T, N), F32),
        scratch_shapes=[pltpu.VMEM((tm, D), BF16)],
        compiler_params=_cparams(("parallel", "arbitrary")),
        name="norm_matmul",
    )(x, gain.reshape(1, D), w)


def _rope_kernel(zq_ref, kc_ref, ks_ref, kw_ref, cos_ref, sin_ref,
                 q_o, kvc_o, kvs_o, kvw_o, ksg_o, vsg_o, kwg_o, vwg_o):
    cos = cos_ref[...]
    sin = sin_ref[...]

    def rot(x):
        w = x.shape[1]
        n = w // LANES
        c = jnp.concatenate([cos] * n, axis=1) if n > 1 else cos
        s = jnp.concatenate([sin] * n, axis=1) if n > 1 else sin
        lane = lax.broadcasted_iota(jnp.int32, x.shape, 1)
        first = (lane % HEAD_DIM) < (HEAD_DIM // 2)
        sw = jnp.where(first, pltpu.roll(x, w - HEAD_DIM // 2, 1), pltpu.roll(x, HEAD_DIM // 2, 1))
        return x * c + sw * s

    kd = N_KV_HEADS * HEAD_DIM
    q_o[0] = (rot(zq_ref[0]) * (HEAD_DIM ** -0.5)).astype(BF16)
    outs = []
    for src, dst in ((kc_ref, kvc_o), (ks_ref, kvs_o), (kw_ref, kvw_o)):
        x = src[0]
        k = rot(x[:, :kd])
        v = x[:, kd:]
        dst[0] = jnp.concatenate([k, v], axis=1)
        outs.append((k, v))
    for g in range(N_KV_HEADS):
        sl = slice(g * HEAD_DIM, (g + 1) * HEAD_DIM)
        ksg_o[0, g] = outs[1][0][:, sl].astype(BF16)
        vsg_o[0, g] = outs[1][1][:, sl].astype(BF16)
        kwg_o[0, g] = outs[2][0][:, sl].astype(BF16)
        vwg_o[0, g] = outs[2][1][:, sl].astype(BF16)


def rope_split(z, cos, sin, tm):
    B, S, _ = z.shape
    G, Dh = N_KV_HEADS, HEAD_DIM
    nt = S // tm
    zspec = lambda w, col: pl.BlockSpec((1, tm, w), lambda b, i: (b, i, col // w))
    tab = pl.BlockSpec((tm, LANES), lambda b, i: (i, 0))
    gspec = pl.BlockSpec((1, G, tm, Dh), lambda b, i: (b, 0, i, 0))
    return pl.pallas_call(
        _rope_kernel,
        grid=(B, nt),
        in_specs=[zspec(D_MODEL, COL_ZQ), zspec(KV_W, COL_KVC), zspec(KV_W, COL_KVS), zspec(KV_W, COL_KVW),
                  tab, tab],
        out_specs=[pl.BlockSpec((1, tm, D_MODEL), lambda b, i: (b, i, 0))]
                  + [pl.BlockSpec((1, tm, KV_W), lambda b, i: (b, i, 0))] * 3 + [gspec] * 4,
        out_shape=[jax.ShapeDtypeStruct((B, S, D_MODEL), BF16)]
                  + [jax.ShapeDtypeStruct((B, S, KV_W), F32)] * 3
                  + [jax.ShapeDtypeStruct((B, G, S, Dh), BF16)] * 4,
        compiler_params=_cparams(("parallel", "parallel")),
        name="rope_split",
    )(z, z, z, z, cos, sin)


def rope_tables(pos):
    half = HEAD_DIM // 2
    inv = ROPE_THETA ** (-jnp.arange(half, dtype=F32) / half)
    ang = pos.astype(F32)[:, None] * inv[None, :]
    cos, sin = jnp.cos(ang), jnp.sin(ang)
    cos_t = jnp.concatenate([cos, cos] * (LANES // HEAD_DIM), axis=1)
    sin_t = jnp.concatenate([-sin, sin] * (LANES // HEAD_DIM), axis=1)
    return cos_t, sin_t


def _rg_gates(xr, wa_ref, ba_ref, wi_ref, bi_ref, lam_ref):
    xb = xr.astype(BF16)
    nblk = wa_ref.shape[0]
    bw = wa_ref.shape[1]
    ga = jnp.concatenate([_dot(xb[:, m * bw:(m + 1) * bw], wa_ref[m]) for m in range(nblk)], axis=1)
    gi = jnp.concatenate([_dot(xb[:, m * bw:(m + 1) * bw], wi_ref[m]) for m in range(nblk)], axis=1)
    r = jax.nn.sigmoid(ga + ba_ref[...])
    i = jax.nn.sigmoid(gi + bi_ref[...])
    nl = -lam_ref[...]
    softplus = jnp.maximum(nl, 0.0) + jnp.log1p(jnp.exp(-jnp.abs(nl)))
    log_a = -RG_C * r * softplus
    a = jnp.exp(log_a)
    bx = jnp.sqrt(-jnp.tanh(log_a) * (a * a + 1.0)) * (i * xr)
    return a, bx


def _rglru_kernel(zr_ref, zg_ref, cw_ref, cb_ref, wa_ref, ba_ref, wi_ref, bi_ref, lam_ref,
                  ya_o, h_o, xs, h_sc, a_sc, b_sc, hs_sc, *, tt):
    t = pl.program_id(1)

    @pl.when(t == 0)
    def _():
        xs[0:8, :] = jnp.zeros((8, xs.shape[1]), F32)
        h_sc[...] = jnp.zeros(h_sc.shape, F32)

    x = zr_ref[0]
    xs[8:8 + tt, :] = x
    w = cw_ref[...]
    nk = w.shape[0]
    xr = cb_ref[...] + w[nk - 1:nk] * x
    for k in range(nk - 1):
        sh = nk - 1 - k
        xr = xr + w[k:k + 1] * xs[8 - sh:8 - sh + tt, :]
    xs[0:8, :] = xs[tt:tt + 8, :]

    a, bx = _rg_gates(xr, wa_ref, ba_ref, wi_ref, bi_ref, lam_ref)
    a_sc[...] = a
    b_sc[...] = bx

    def body(i, h):
        r0 = pl.multiple_of(i * 8, 8)
        a8 = a_sc[pl.ds(r0, 8), :]
        b8 = b_sc[pl.ds(r0, 8), :]
        rows = []
        for k in range(8):
            h = a8[k:k + 1] * h + b8[k:k + 1]
            rows.append(h)
        hs_sc[pl.ds(r0, 8), :] = jnp.concatenate(rows, axis=0)
        return h

    h = lax.fori_loop(0, tt // 8, body, h_sc[0:1, :])
    h_sc[0:1, :] = h
    h_o[0] = h
    ya_o[0] = (jax.nn.gelu(zg_ref[0]) * hs_sc[...]).astype(BF16)


def rglru_prompt(z, cw, cb, wa, ba, wi, bi, lam, tt):
    B, S, _ = z.shape
    D = D_MODEL
    nblk, bw = wa.shape[0], wa.shape[1]
    vec = pl.BlockSpec((1, D), lambda b, t: (0, 0))
    wspec = pl.BlockSpec((nblk, bw, bw), lambda b, t: (0, 0, 0))
    return pl.pallas_call(
        functools.partial(_rglru_kernel, tt=tt),
        grid=(B, S // tt),
        in_specs=[pl.BlockSpec((1, tt, D), lambda b, t: (b, t, COL_ZR // D)),
                  pl.BlockSpec((1, tt, D), lambda b, t: (b, t, COL_ZG // D)),
                  pl.BlockSpec(cw.shape, lambda b, t: (0, 0)), vec, wspec, vec, wspec, vec, vec],
        out_specs=[pl.BlockSpec((1, tt, D), lambda b, t: (b, t, 0)),
                   pl.BlockSpec((1, 1, D), lambda b, t: (b, 0, 0))],
        out_shape=[jax.ShapeDtypeStruct((B, S, D), BF16), jax.ShapeDtypeStruct((B, 1, D), F32)],
        scratch_shapes=[pltpu.VMEM((tt + 8, D), F32), pltpu.VMEM((8, D), F32),
                        pltpu.VMEM((tt, D), F32), pltpu.VMEM((tt, D), F32), pltpu.VMEM((tt, D), F32)],
        compiler_params=_cparams(("parallel", "arbitrary")),
        name="rglru_prompt",
    )(z, z, cw, cb, wa, ba, wi, bi, lam)


CF_PAD = 32
CF_ROWS = 64


def _layernorm_silu(u, g, b):
    mu = jnp.mean(u, axis=-1, keepdims=True)
    d = u - mu
    var = jnp.mean(d * d, axis=-1, keepdims=True)
    y = d * lax.rsqrt(var + EPS) * g + b
    return y * jax.nn.sigmoid(y)


def _conformer_kernel(za_ref, zb_ref, w_ref, b_ref, g_ref, be_ref, yb_o, st_o, us, u_sc, *, tt):
    t = pl.program_id(1)
    D = us.shape[1]

    @pl.when(t == 0)
    def _():
        us[0:CF_PAD, :] = jnp.zeros((CF_PAD, D), F32)

    us[CF_PAD:CF_PAD + tt, :] = za_ref[0] * jax.nn.sigmoid(zb_ref[0])
    width = w_ref.shape[0]
    off = CF_PAD - (width - 1)
    for c in range(D // LANES):
        cs = slice(c * LANES, (c + 1) * LANES)
        for rs in range(tt // CF_ROWS):
            acc = jnp.broadcast_to(b_ref[:, cs], (CF_ROWS, LANES))
            for k in range(width):
                r0 = off + k + rs * CF_ROWS
                acc = acc + w_ref[k:k + 1, cs] * us[r0:r0 + CF_ROWS, cs]
            u_sc[rs * CF_ROWS:(rs + 1) * CF_ROWS, cs] = acc
    tail = us[tt:tt + CF_PAD, :]
    st_o[0] = tail
    us[0:CF_PAD, :] = tail
    yb_o[0] = _layernorm_silu(u_sc[...], g_ref[...], be_ref[...]).astype(BF16)


def conformer_prompt(z, w, b, g, be, tt):
    B, S, _ = z.shape
    D = D_MODEL
    vec = pl.BlockSpec((1, D), lambda bb, t: (0, 0))
    return pl.pallas_call(
        functools.partial(_conformer_kernel, tt=tt),
        grid=(B, S // tt),
        in_specs=[pl.BlockSpec((1, tt, D), lambda bb, t: (bb, t, COL_ZA // D)),
                  pl.BlockSpec((1, tt, D), lambda bb, t: (bb, t, COL_ZB // D)),
                  pl.BlockSpec(w.shape, lambda bb, t: (0, 0)), vec, vec, vec],
        out_specs=[pl.BlockSpec((1, tt, D), lambda bb, t: (bb, t, 0)),
                   pl.BlockSpec((1, CF_PAD, D), lambda bb, t: (bb, 0, 0))],
        out_shape=[jax.ShapeDtypeStruct((B, S, D), BF16), jax.ShapeDtypeStruct((B, CF_PAD, D), F32)],
        scratch_shapes=[pltpu.VMEM((tt + CF_PAD, D), F32), pltpu.VMEM((tt, D), F32)],
        compiler_params=_cparams(("parallel", "arbitrary")),
        name="conformer_prompt",
    )(z, z, w, b, g, be)


def _sample_mix_kernel(zr_ref, zg_ref, za_ref, zb_ref, rgst_ref, h0_ref, cfst_ref,
                       cw_ref, cb_ref, wa_ref, ba_ref, wi_ref, bi_ref, lam_ref,
                       fw_ref, fb_ref, fg_ref, fbe_ref,
                       ya_o, yb_o, h_o, u_o):
    x = zr_ref[...]
    w = cw_ref[...]
    nk = w.shape[0]
    xr = cb_ref[...] + w[nk - 1:nk] * x
    for k in range(nk - 1):
        xr = xr + w[k:k + 1] * rgst_ref[k]
    a, bx = _rg_gates(xr, wa_ref, ba_ref, wi_ref, bi_ref, lam_ref)
    h = a * h0_ref[...] + bx
    h_o[...] = h
    ya_o[...] = (jax.nn.gelu(zg_ref[...]) * h).astype(BF16)

    u_new = za_ref[...] * jax.nn.sigmoid(zb_ref[...])
    u_o[...] = u_new
    width = fw_ref.shape[0]
    acc = fb_ref[...] + fw_ref[width - 1:width, :] * u_new
    for k in range(width - 1):
        acc = acc + fw_ref[k:k + 1, :] * cfst_ref[k]
    yb_o[...] = _layernorm_silu(acc, fg_ref[...], fbe_ref[...]).astype(BF16)


def sample_mix(z, rgst, h0, cfst, cw, cb, wa, ba, wi, bi, lam, fw, fb, fg, fbe):
    DB = z.shape[0]
    D = D_MODEL
    full = lambda a: pl.BlockSpec(a.shape, lambda i: (0,) * a.ndim)
    zs = lambda col: pl.BlockSpec((DB, D), lambda i: (0, col // D))
    o = pl.BlockSpec((DB, D), lambda i: (0, 0))
    return pl.pallas_call(
        _sample_mix_kernel,
        grid=(1,),
        in_specs=[zs(COL_ZR), zs(COL_ZG), zs(COL_ZA), zs(COL_ZB), full(rgst), full(h0), full(cfst),
                  full(cw), full(cb), full(wa), full(ba), full(wi), full(bi), full(lam),
                  full(fw), full(fb), full(fg), full(fbe)],
        out_specs=[o, o, o, o],
        out_shape=[jax.ShapeDtypeStruct((DB, D), BF16), jax.ShapeDtypeStruct((DB, D), BF16),
                   jax.ShapeDtypeStruct((DB, D), F32), jax.ShapeDtypeStruct((DB, D), F32)],
        compiler_params=_cparams(("arbitrary",)),
        name="sample_mix",
    )(z, z, z, z, rgst, h0, cfst, cw, cb, wa, ba, wi, bi, lam, fw, fb, fg, fbe)


CMP_PAGES = 16


def _compress_kernel(pages_ref, *refs):
    del pages_ref
    x_refs, w_ref, o_ref = refs[:CMP_PAGES], refs[CMP_PAGES], refs[CMP_PAGES + 1]
    x = jnp.concatenate([r[...] for r in x_refs], axis=0)
    o_ref[...] = _dot(x, w_ref[...])


def compress_pages(xt, pages, w):
    n = pages.shape[0]
    rows, kdim = xt.shape[1], xt.shape[2]
    specs = [pl.BlockSpec((None, rows, kdim), (lambda s, pg, i=i: (pg[s * CMP_PAGES + i], 0, 0)))
             for i in range(CMP_PAGES)]
    grid_spec = pltpu.PrefetchScalarGridSpec(
        num_scalar_prefetch=1,
        grid=(n // CMP_PAGES,),
        in_specs=specs + [pl.BlockSpec(w.shape, lambda s, pg: (0, 0))],
        out_specs=pl.BlockSpec((CMP_PAGES * rows, w.shape[1]), lambda s, pg: (s, 0)),
    )
    return pl.pallas_call(
        _compress_kernel,
        grid_spec=grid_spec,
        out_shape=jax.ShapeDtypeStruct((n * rows, w.shape[1]), F32),
        compiler_params=_cparams(("arbitrary",)),
        name="compress_pages",
    )(pages, *([xt] * CMP_PAGES), w)


def pages_relayout(x):
    NP = x.shape[0]
    G, Dh = N_KV_HEADS, HEAD_DIM
    x = x.reshape(NP, 2, CMP_BLOCK, 2, G, Dh)
    x = jnp.transpose(x, (0, 3, 1, 4, 2, 5))
    return x.reshape(NP, 4 * G, CMP_BLOCK * Dh).astype(BF16)


def split_ck_cv(o, lead):
    G, Dh = N_KV_HEADS, HEAD_DIM
    o = o.reshape(lead + (2, 2, G, 2 * Dh))
    ck = o[..., 0, :, :, :Dh]
    cv = o[..., 1, :, :, Dh:]
    return ck, cv


def _softmax_rows(s):
    m = jnp.max(s, axis=-1, keepdims=True)
    e = jnp.exp(s - m)
    return e / jnp.sum(e, axis=-1, keepdims=True)


def _rank_select(v, n_sel):
    n = v.shape[0]
    iidx = lax.broadcasted_iota(jnp.int32, v.shape, 0)
    rank = jnp.zeros(v.shape, F32)
    for j in range(n):
        rj = v[j:j + 1]
        tie = jnp.where(iidx > j, 1.0, 0.0)
        rank = rank + jnp.where(rj > v, 1.0, jnp.where(rj == v, tie, 0.0))
    return jnp.where(rank < n_sel, 1.0, 0.0)


def _attn_prompt_kernel(q_ref, ck_ref, cv_ref, ks_ref, vs_ref, kw_ref, vw_ref, ng_ref, o_ref,
                        *, QB, KB, NB):
    R, Dh = Q_PER_KV, HEAD_DIM
    c = pl.program_id(2)
    pos0 = c * QB
    qblk = q_ref[0]
    q4 = jnp.concatenate([qblk[:, r * Dh:(r + 1) * Dh] for r in range(R)], axis=0)
    qp = pos0 + lax.broadcasted_iota(jnp.int32, (QB, 1), 0)
    qp4 = jnp.concatenate([qp] * R, axis=0)

    NBP = ck_ref.shape[2]
    s = _dot_nt(q4, ck_ref[0, 0])
    blk = lax.broadcasted_iota(jnp.int32, (1, NBP), 1)
    cmask = ((blk + 1) * CMP_BLOCK - 1) <= qp4
    p = _softmax_rows(jnp.where(cmask, s, NEG_INF)) * jnp.where(cmask, 1.0, 0.0)
    o_cmp = _dot(p.astype(BF16), cv_ref[0, 0])
    imp = p[0:QB] + p[QB:2 * QB] + p[2 * QB:3 * QB] + p[3 * QB:4 * QB]
    imp = jnp.where(blk == qp // CMP_BLOCK, FORCE_LOCAL,
                    jnp.where(blk == 0, FORCE_INIT, jnp.where(blk * CMP_BLOCK <= qp, imp, -1.0)))
    sel_t = _rank_select(imp.T[:NB], N_SEL)
    if NBP > NB:
        sel_t = jnp.concatenate([sel_t, jnp.zeros((NBP - NB, QB), F32)], axis=0)
    sel = sel_t.T.astype(BF16)

    def flash(k_ref, v_ref, lo, hi, allowed_fn):
        def body(kt, carry):
            m, l, acc = carry
            k0 = pl.multiple_of(kt * KB, KB)
            k_t = k_ref[0, 0, pl.ds(k0, KB), :]
            v_t = v_ref[0, 0, pl.ds(k0, KB), :]
            sc = _dot_nt(q4, k_t)
            alw = allowed_fn(k0)
            alw4 = jnp.concatenate([alw] * R, axis=0)
            sc = jnp.where(alw4 > 0.5, sc, NEG_INF)
            m_new = jnp.maximum(m, jnp.max(sc, axis=-1, keepdims=True))
            pe = jnp.exp(sc - m_new) * alw4
            alpha = jnp.exp(m - m_new)
            l = alpha * l + jnp.sum(pe, axis=-1, keepdims=True)
            acc = alpha * acc + _dot(pe.astype(BF16), v_t)
            return m_new, l, acc

        init = (jnp.full((R * QB, 1), NEG_INF, F32), jnp.zeros((R * QB, 1), F32),
                jnp.zeros((R * QB, Dh), F32))
        _, l, acc = lax.fori_loop(lo, hi, body, init)
        return acc / l

    def sel_allowed(k0):
        key = k0 + lax.broadcasted_iota(jnp.int32, (1, KB), 1)
        nrow = lax.broadcasted_iota(jnp.int32, (NBP, KB), 0)
        expand = jnp.where(nrow == key // CMP_BLOCK, 1.0, 0.0).astype(BF16)
        return _dot(sel, expand) * jnp.where(key <= qp, 1.0, 0.0)

    def win_allowed(k0):
        key = k0 + lax.broadcasted_iota(jnp.int32, (1, KB), 1)
        d = qp - key
        return jnp.where((d >= 0) & (d <= WINDOW), 1.0, 0.0)

    hi = (pos0 + QB + KB - 1) // KB
    o_slc = flash(ks_ref, vs_ref, 0, hi, sel_allowed)
    o_win = flash(kw_ref, vw_ref, jnp.maximum(pos0 - WINDOW, 0) // KB, hi, win_allowed)

    gates = jax.nn.sigmoid(ng_ref[0])
    outs = []
    for r in range(R):
        rs = slice(r * QB, (r + 1) * QB)
        outs.append(gates[:, 3 * r:3 * r + 1] * o_cmp[rs] + gates[:, 3 * r + 1:3 * r + 2] * o_slc[rs]
                    + gates[:, 3 * r + 2:3 * r + 3] * o_win[rs])
    o_ref[0] = jnp.concatenate(outs, axis=1).astype(BF16)


def attn_prompt(q, ck, cv, ksg, vsg, kwg, vwg, z, QB, KB):
    B, S, _ = q.shape
    G, R, Dh = N_KV_HEADS, Q_PER_KV, HEAD_DIM
    NBP = ck.shape[2]
    NB = S // CMP_BLOCK
    cspec = pl.BlockSpec((1, 1, NBP, Dh), lambda b, g, c: (b, g, 0, 0))
    kspec = pl.BlockSpec((1, 1, S, Dh), lambda b, g, c: (b, g, 0, 0))
    return pl.pallas_call(
        functools.partial(_attn_prompt_kernel, QB=QB, KB=KB, NB=NB),
        grid=(B, G, S // QB),
        in_specs=[pl.BlockSpec((1, QB, R * Dh), lambda b, g, c: (b, c, g)),
                  cspec, cspec, kspec, kspec, kspec, kspec,
                  pl.BlockSpec((1, QB, LANES), lambda b, g, c: (b, c, COL_NG // LANES + g))],
        out_specs=pl.BlockSpec((1, QB, R * Dh), lambda b, g, c: (b, c, g)),
        out_shape=jax.ShapeDtypeStruct((B, S, G * R * Dh), BF16),
        compiler_params=_cparams(("parallel", "parallel", "arbitrary")),
        name="attn_prompt",
    )(q, ck, cv, ksg, vsg, kwg, vwg, z)


QPAD = 16


def _pad_rows(x, n):
    return jnp.concatenate([x, jnp.zeros((n - x.shape[0], x.shape[1]), x.dtype)], axis=0)


def _sample_cmp_kernel(q_ref, ck_ref, cv_ref, o_ref, idx_ref, *, NBC):
    G, R = N_KV_HEADS, Q_PER_KV
    imps = []
    for g in range(G):
        q = _pad_rows(q_ref[g], QPAD).astype(BF16)
        p = _softmax_rows(_dot_nt(q, ck_ref[g]))
        o_ref[g] = _dot(p.astype(BF16), cv_ref[g])[:R]
        imps.append(p[0:1] + p[1:2] + p[2:3] + p[3:4])
    v = jnp.concatenate(imps + [jnp.full((8 - G, NBC), -1.0, F32)], axis=0)
    lane = lax.broadcasted_iota(jnp.int32, v.shape, 1)
    v = jnp.where(lane == 0, -1.0, v)
    out_lane = lax.broadcasted_iota(jnp.int32, (8, LANES), 1)
    idx = jnp.where(out_lane == N_SEL - 1, NBC, 0)
    for t in range(1, N_SEL - 1):
        m = jnp.max(v, axis=-1, keepdims=True)
        first = jnp.min(jnp.where(v == m, lane, NBC), axis=-1, keepdims=True)
        idx = jnp.where(out_lane == t, first, idx)
        v = jnp.where(lane == first, -2.0, v)
    idx_ref[...] = idx


def sample_cmp(q, ck, cv):
    DB, G, R, Dh = q.shape
    NBC = ck.shape[2]
    return pl.pallas_call(
        functools.partial(_sample_cmp_kernel, NBC=NBC),
        grid=(DB,),
        in_specs=[pl.BlockSpec((None, G, R, Dh), lambda b: (b, 0, 0, 0)),
                  pl.BlockSpec((None, G, NBC, Dh), lambda b: (b, 0, 0, 0)),
                  pl.BlockSpec((None, G, NBC, Dh), lambda b: (b, 0, 0, 0))],
        out_specs=[pl.BlockSpec((None, G, R, Dh), lambda b: (b, 0, 0, 0)),
                   pl.BlockSpec((None, 8, LANES), lambda b: (b, 0, 0))],
        out_shape=[jax.ShapeDtypeStruct((DB, G, R, Dh), F32), jax.ShapeDtypeStruct((DB, 8, LANES), jnp.int32)],
        compiler_params=_cparams(("parallel",)),
        name="sample_cmp",
    )(q, ck, cv)


N_GATHER = N_SEL - 1


def _sample_attn_kernel(idx_ref, pt_ref, *refs):
    del idx_ref, pt_ref
    R, Dh = Q_PER_KV, HEAD_DIM
    blk_refs = refs[:N_GATHER]
    q_ref, ns_ref, win_ref, nw_ref, oc_ref, ng_ref, o_ref = refs[N_GATHER:]
    q = _pad_rows(q_ref[...], QPAD).astype(BF16)
    row0 = lax.broadcasted_iota(jnp.int32, (CMP_BLOCK, Dh), 0) == 0

    def new_block(n_ref, kv):
        return jnp.where(row0, jnp.broadcast_to(n_ref[kv], (CMP_BLOCK, Dh)), 0.0).astype(BF16)

    def attend(kv_past, n_ref):
        k = jnp.concatenate([kv_past[:, :Dh], new_block(n_ref, 0)], axis=0)
        v = jnp.concatenate([kv_past[:, Dh:], new_block(n_ref, 1)], axis=0)
        nk = k.shape[0]
        sc = _dot_nt(q, k)
        key = lax.broadcasted_iota(jnp.int32, (1, nk), 1)
        sc = jnp.where(key <= nk - CMP_BLOCK, sc, NEG_INF)
        p = _softmax_rows(sc)
        return _dot(p.astype(BF16), v)[:R]

    o_slc = attend(jnp.concatenate([r[...] for r in blk_refs], axis=0), ns_ref)
    o_win = attend(win_ref[...], nw_ref)
    gates = jax.nn.sigmoid(ng_ref[...])
    o_cmp = oc_ref[...]
    rows = []
    for r in range(R):
        rows.append(gates[:, 3 * r:3 * r + 1] * o_cmp[r:r + 1] + gates[:, 3 * r + 1:3 * r + 2] * o_slc[r:r + 1]
                    + gates[:, 3 * r + 2:3 * r + 3] * o_win[r:r + 1])
    o_ref[...] = jnp.concatenate(rows, axis=0)


def sample_attn(idx, page_table, slct, q, new_s, wint, new_w, o_cmp, ng):
    DB, G, R, Dh = q.shape
    npg = page_table.shape[1]
    W = wint.shape[2]
    pt = page_table.reshape(-1)

    def blk_map(k):
        def im(b, g, idx_r, pt_r):
            n = idx_r[(b * G + g) * N_SEL + k]
            return (2 * pt_r[b * npg + n // 2] + n % 2, g, 0, 0)
        return im

    bg = lambda *tail: (lambda b, g, i, p: (b, g) + tail)
    grid_spec = pltpu.PrefetchScalarGridSpec(
        num_scalar_prefetch=2,
        grid=(DB, G),
        in_specs=[pl.BlockSpec((None, None, CMP_BLOCK, 2 * Dh), blk_map(k)) for k in range(N_GATHER)]
                 + [pl.BlockSpec((None, None, R, Dh), bg(0, 0)),
                    pl.BlockSpec((None, None, 2, 1, Dh), bg(0, 0, 0)),
                    pl.BlockSpec((None, None, W, 2 * Dh), bg(0, 0)),
                    pl.BlockSpec((None, None, 2, 1, Dh), bg(0, 0, 0)),
                    pl.BlockSpec((None, None, R, Dh), bg(0, 0)),
                    pl.BlockSpec((None, None, 1, LANES), bg(0, 0))],
        out_specs=pl.BlockSpec((None, None, R, Dh), bg(0, 0)),
    )
    return pl.pallas_call(
        _sample_attn_kernel,
        grid_spec=grid_spec,
        out_shape=jax.ShapeDtypeStruct((DB, G, R, Dh), F32),
        compiler_params=_cparams(("arbitrary", "arbitrary")),
        name="sample_attn",
    )(idx, pt, *([slct] * N_GATHER), q, new_s, wint, new_w, o_cmp, ng)


def _merge_kernel(x_ref, ya_ref, yb_ref, yc_ref, m0_ref, m1_ref, m2_ref, wa_ref, wb_ref, wc_ref, wo_ref, o_ref):
    m = (jax.nn.sigmoid(m0_ref[...]) * _dot(ya_ref[...], wa_ref[...])
         + jax.nn.sigmoid(m1_ref[...]) * _dot(yb_ref[...], wb_ref[...])
         + jax.nn.sigmoid(m2_ref[...]) * _dot(yc_ref[...], wc_ref[...]))
    o_ref[...] = x_ref[...] + _dot(m.astype(BF16), wo_ref[...])


def merge_out(x, ya, yb, yc, z, wa, wb, wc, wo, tm):
    T, D = x.shape
    row = pl.BlockSpec((tm, D), lambda i: (i, 0))
    zs = lambda k: pl.BlockSpec((tm, D), lambda i: (i, COL_MG // D + k))
    wsp = pl.BlockSpec((D, D), lambda i: (0, 0))
    return pl.pallas_call(
        _merge_kernel,
        grid=(T // tm,),
        in_specs=[row, row, row, row, zs(0), zs(1), zs(2), wsp, wsp, wsp, wsp],
        out_specs=row,
        out_shape=jax.ShapeDtypeStruct((T, D), F32),
        compiler_params=_cparams(("parallel",)),
        name="merge_out",
    )(x, ya, yb, yc, z, z, z, wa, wb, wc, wo)


def _ffn_kernel(x_ref, g_ref, wg_ref, wu_ref, wd_ref, o_ref, hn_sc, acc_sc):
    j = pl.program_id(1)

    @pl.when(j == 0)
    def _():
        x = x_ref[...]
        ms = jnp.mean(x * x, axis=-1, keepdims=True)
        hn_sc[...] = (x * lax.rsqrt(ms + EPS) * g_ref[...]).astype(BF16)
        acc_sc[...] = jnp.zeros(acc_sc.shape, F32)

    hn = hn_sc[...]
    fg = _dot(hn, wg_ref[...])
    fu = _dot(hn, wu_ref[...])
    act = (fg * jax.nn.sigmoid(fg)) * fu
    acc_sc[...] += _dot(act.astype(BF16), wd_ref[...])

    @pl.when(j == pl.num_programs(1) - 1)
    def _():
        o_ref[...] = x_ref[...] + acc_sc[...]


def ffn(x, gain, w_gu, w_down, tm, fc):
    T, D = x.shape
    F = w_down.shape[0]
    nf = F // fc
    return pl.pallas_call(
        _ffn_kernel,
        grid=(T // tm, nf),
        in_specs=[pl.BlockSpec((tm, D), lambda i, j: (i, 0)),
                  pl.BlockSpec((1, D), lambda i, j: (0, 0)),
                  pl.BlockSpec((D, fc), lambda i, j: (0, j)),
                  pl.BlockSpec((D, fc), lambda i, j: (0, nf + j)),
                  pl.BlockSpec((fc, D), lambda i, j: (j, 0))],
        out_specs=pl.BlockSpec((tm, D), lambda i, j: (i, 0)),
        out_shape=jax.ShapeDtypeStruct((T, D), F32),
        scratch_shapes=[pltpu.VMEM((tm, D), BF16), pltpu.VMEM((tm, D), F32)],
        compiler_params=_cparams(("parallel", "arbitrary")),
        name="ffn",
    )(x, gain.reshape(1, D), w_gu, w_gu, w_down)


def _rmsnorm_kernel(x_ref, g_ref, o_ref):
    x = x_ref[...]
    ms = jnp.mean(x * x, axis=-1, keepdims=True)
    o_ref[...] = x * lax.rsqrt(ms + EPS) * g_ref[...]


def rmsnorm_rows(x, gain, tm):
    T, D = x.shape
    return pl.pallas_call(
        _rmsnorm_kernel,
        grid=(T // tm,),
        in_specs=[pl.BlockSpec((tm, D), lambda i: (i, 0)), pl.BlockSpec((1, D), lambda i: (0, 0))],
        out_specs=pl.BlockSpec((tm, D), lambda i: (i, 0)),
        out_shape=jax.ShapeDtypeStruct((T, D), F32),
        compiler_params=_cparams(("parallel",)),
        name="final_rmsnorm",
    )(x, gain.reshape(1, D))


def _prep_w_in(w):
    D = w.shape[0]
    G, R = N_KV_HEADS, Q_PER_KV
    n_gate = 3 * G * R
    head, gates, tail = w[:, :COL_NG], w[:, COL_NG:COL_NG + n_gate], w[:, COL_NG + n_gate:]
    gates = gates.reshape(D, G, 3 * R)
    gates = jnp.pad(gates, ((0, 0), (0, 0), (0, LANES - 3 * R))).reshape(D, G * LANES)
    return jnp.concatenate([head, gates, tail], axis=1).astype(BF16)


RG_TILE = 256


def _prep_blockdiag(w):
    n, c, _ = w.shape
    per = RG_TILE // c
    w = w.reshape(n // per, per, c, c)
    eye = jnp.eye(per, dtype=w.dtype)
    t = jnp.einsum('mpcd,pq->mpcqd', w, eye)
    return t.reshape(n // per, RG_TILE, RG_TILE).astype(BF16)


def _tile(n, pref):
    t = min(n, pref)
    assert n % t == 0, (n, t)
    return t


def kernel(x_prompt, x_sample, cache_cmp_kv, cache_slc_kv, cache_win_kv, state_rglru_conv, state_rglru_h,
           state_conf_conv, page_table, norm1_g, w_in, rg_conv_w, rg_conv_b, rg_wa, rg_ba, rg_wi, rg_bi,
           rg_lambda, cf_dw_w, cf_dw_b, cf_ln_g, cf_ln_b, nsa_w_ck, nsa_w_cv, w_br_rnn, w_br_conv, w_br_attn,
           w_out, norm2_g, w_ffn_gu, w_ffn_down, final_norm_g):
    B, S, D = x_prompt.shape
    DB = x_sample.shape[0]
    depth = w_in.shape[0]
    G, R, Dh = N_KV_HEADS, Q_PER_KV, HEAD_DIM
    n_pages = page_table.shape[1]
    page_rows = cache_cmp_kv.shape[2]
    past_len = n_pages * page_rows
    w_buf = cache_win_kv.shape[2]
    n_phys = cache_cmp_kv.shape[1]
    d_ff = w_ffn_down.shape[1]
    rg_k = rg_conv_w.shape[1]
    cf_k = cf_dw_w.shape[1]
    assert D == D_MODEL and x_sample.shape[1] == 1 and page_rows == 2 * CMP_BLOCK
    assert cache_cmp_kv.shape[3:] == (2, G, Dh) and w_buf == min(WINDOW, past_len)
    assert S % (2 * CMP_BLOCK) == 0 and past_len % CMP_BLOCK == 0 and cf_k - 1 <= CF_PAD

    T = B * S
    tm_p = _tile(T, 1024)
    tm_rope = _tile(S, 512)
    tt = _tile(S, 256)
    QB = _tile(S, 128)
    KB = _tile(S, 256)
    fc = d_ff // 2 if (d_ff // 2) % LANES == 0 else d_ff

    cos_p, sin_p = rope_tables(jnp.arange(S, dtype=jnp.int32))
    cos_s, sin_s = rope_tables(jnp.full((DB,), past_len, jnp.int32))
    prompt_pages = jnp.arange(T // page_rows, dtype=jnp.int32)
    sample_pages = page_table.reshape(-1).astype(jnp.int32)
    row2 = lambda v: v.reshape(1, -1)

    xp = x_prompt.reshape(T, D)
    xs = x_sample.reshape(DB, D)
    st_p, st_s = [], []
    for l in range(depth):
        w_in_l = _prep_w_in(w_in[l])
        wa_l, wi_l = _prep_blockdiag(rg_wa[l]), _prep_blockdiag(rg_wi[l])
        w_cmp = jnp.concatenate([nsa_w_ck[l].reshape(CMP_BLOCK * Dh, Dh),
                                 nsa_w_cv[l].reshape(CMP_BLOCK * Dh, Dh)], axis=1).astype(BF16)
        w_a, w_b, w_c = w_br_rnn[l].astype(BF16), w_br_conv[l].astype(BF16), w_br_attn[l].astype(BF16)
        w_o, w_gu, w_dn = w_out[l].astype(BF16), w_ffn_gu[l].astype(BF16), w_ffn_down[l].astype(BF16)
        rg_args = (rg_conv_w[l], row2(rg_conv_b[l]), wa_l, row2(rg_ba[l]), wi_l, row2(rg_bi[l]),
                   row2(rg_lambda[l]))
        cf_args = (cf_dw_w[l], row2(cf_dw_b[l]), row2(cf_ln_g[l]), row2(cf_ln_b[l]))

        z = norm_matmul(xp, norm1_g[l], w_in_l, tm_p, 1024).reshape(B, S, N_PAD)
        ya, h_last = rglru_prompt(z, *rg_args, tt)
        yb, cf_tail = conformer_prompt(z, *cf_args, tt)
        q, kvc, kvs, kvw, ksg, vsg, kwg, vwg = rope_split(z, cos_p, sin_p, tm_rope)
        cko = compress_pages(pages_relayout(kvc.reshape(T // page_rows, page_rows, 2, G, Dh)), prompt_pages, w_cmp)
        ck, cv = split_ck_cv(cko, (B, S // page_rows))
        nb = S // CMP_BLOCK
        nbp = -(-nb // LANES) * LANES

        def to_groups(a):
            a = jnp.transpose(a.reshape(B, nb, G, Dh), (0, 2, 1, 3))
            return jnp.pad(a, ((0, 0), (0, 0), (0, nbp - nb), (0, 0))).astype(BF16)

        yc = attn_prompt(q, to_groups(ck), to_groups(cv), ksg, vsg, kwg, vwg, z, QB, KB)
        xp = merge_out(xp, ya.reshape(T, D), yb.reshape(T, D), yc.reshape(T, D), z.reshape(T, N_PAD),
                       w_a, w_b, w_c, w_o, _tile(T, 512))
        xp = ffn(xp, norm2_g[l], w_gu, w_dn, _tile(T, 512), fc)
        kv6 = lambda a: a.reshape(B, S, 2, G, Dh)
        st_p.append((kv6(kvc), kv6(kvs), kv6(kvw)[:, S - w_buf:] if S >= w_buf else
                     jnp.pad(kv6(kvw), ((0, 0), (w_buf - S, 0), (0, 0), (0, 0), (0, 0))),
                     z[:, S - (rg_k - 1):, COL_ZR:COL_ZR + D], h_last.reshape(B, D),
                     cf_tail[:, CF_PAD - (cf_k - 1):]))

        zs = norm_matmul(xs, norm1_g[l], w_in_l, DB, 1024)
        rgst = jnp.transpose(state_rglru_conv[l], (1, 0, 2))
        cfst = jnp.transpose(state_conf_conv[l], (1, 0, 2))
        ya_s, yb_s, h_s, u_s = sample_mix(zs, rgst, state_rglru_h[l], cfst, *rg_args, *cf_args)
        q_s, kvc_s, kvs_s, kvw_s, _, _, _, _ = rope_split(zs.reshape(1, DB, N_PAD), cos_s, sin_s, DB)
        q_s = q_s.reshape(DB, G, R, Dh).astype(F32)
        cko_s = compress_pages(pages_relayout(cache_cmp_kv[l]), sample_pages, w_cmp)
        ck_s, cv_s = split_ck_cv(cko_s, (DB, n_pages))
        nbc = past_len // CMP_BLOCK
        to_g = lambda a: jnp.transpose(a.reshape(DB, nbc, G, Dh), (0, 2, 1, 3)).astype(BF16)
        o_cmp_s, idx = sample_cmp(q_s, to_g(ck_s), to_g(cv_s))
        idx = idx[:, :G, :N_SEL].reshape(-1)
        slct = cache_slc_kv[l].reshape(n_phys, 2, CMP_BLOCK, 2, G, Dh)
        slct = jnp.transpose(slct, (0, 1, 4, 2, 3, 5)).reshape(n_phys * 2, G, CMP_BLOCK, 2 * Dh).astype(BF16)
        wint = jnp.transpose(cache_win_kv[l], (0, 3, 1, 2, 4)).reshape(DB, G, w_buf, 2 * Dh).astype(BF16)
        new_rows = lambda a: jnp.transpose(a.reshape(DB, 2, G, 1, Dh), (0, 2, 1, 3, 4))
        ng_s = zs[:, COL_NG:COL_NG + G * LANES].reshape(DB, G, 1, LANES)
        yc_s = sample_attn(idx, page_table.astype(jnp.int32), slct, q_s, new_rows(kvs_s[0]), wint,
                           new_rows(kvw_s[0]), o_cmp_s, ng_s)
        xs = merge_out(xs, ya_s, yb_s, yc_s.reshape(DB, D).astype(BF16), zs, w_a, w_b, w_c, w_o, DB)
        xs = ffn(xs, norm2_g[l], w_gu, w_dn, DB, fc)
        kv6s = lambda a: a.reshape(DB, 1, 2, G, Dh)
        st_s.append((kv6s(kvc_s), kv6s(kvs_s),
                     jnp.concatenate([cache_win_kv[l], kv6s(kvw_s)], axis=1)[:, 1:],
                     jnp.concatenate([state_rglru_conv[l], zs[:, None, COL_ZR:COL_ZR + D]], axis=1)[:, 1:],
                     h_s,
                     jnp.concatenate([state_conf_conv[l], u_s[:, None]], axis=1)[:, 1:]))

    y_prompt = rmsnorm_rows(xp, final_norm_g, tm_p).reshape(B, S, D)
    y_sample = rmsnorm_rows(xs, final_norm_g, DB).reshape(DB, 1, D)
    stk = lambda sts, j: jnp.stack([st[j] for st in sts], axis=0)
    return (y_prompt, y_sample,
            stk(st_p, 0), stk(st_p, 1), stk(st_p, 2), stk(st_p, 3), stk(st_p, 4), stk(st_p, 5),
            stk(st_s, 0), stk(st_s, 1), stk(st_s, 2), stk(st_s, 3), stk(st_s, 4), stk(st_s, 5))
```

```python
import functools

import jax
import jax.numpy as jnp
from jax import lax
from jax.experimental import pallas as pl
from jax.experimental.pallas import tpu as pltpu

F32 = jnp.float32
BF16 = jnp.bfloat16

CMP_BLOCK = 64
N_SEL = 16
WINDOW = 512
FORCE_LOCAL = 1e4
FORCE_INIT = 5e3
RG_C = 8.0
ROPE_THETA = 10000.0
EPS = 1e-6
NEG_INF = -1e30

HEAD_DIM = 64
N_KV_HEADS = 4
Q_PER_KV = 4
LANES = 128
VMEM_LIMIT = 56 * 1024 * 1024

D_MODEL = 1024
COL_ZR, COL_ZG, COL_ZA, COL_ZB, COL_ZQ = 0, 1024, 2048, 3072, 4096
COL_KVC, COL_KVS, COL_KVW = 5120, 5632, 6144
COL_NG = 6656
COL_MG = 7168
N_PAD = COL_MG + 3 * D_MODEL
KV_W = 2 * N_KV_HEADS * HEAD_DIM
PAGE = 2 * CMP_BLOCK


def _cparams(sem):
    return pltpu.CompilerParams(dimension_semantics=sem, vmem_limit_bytes=VMEM_LIMIT)


def _dot(a, b):
    return jnp.dot(a, b, preferred_element_type=F32)


def _dot_nt(a, b):
    return lax.dot_general(a, b, (((1,), (1,)), ((), ())), preferred_element_type=F32)


def _norm_mm_kernel(x_ref, g_ref, w_ref, o_ref, hn_sc):
    @pl.when(pl.program_id(1) == 0)
    def _():
        x = x_ref[...]
        ms = jnp.mean(x * x, axis=-1, keepdims=True)
        hn_sc[...] = (x * lax.rsqrt(ms + EPS) * g_ref[...]).astype(BF16)

    o_ref[...] = _dot(hn_sc[...], w_ref[...])


def norm_matmul(x, gain, w, tm, tn):
    T, D = x.shape
    N = w.shape[1]
    return pl.pallas_call(
        _norm_mm_kernel,
        grid=(T // tm, N // tn),
        in_specs=[pl.BlockSpec((tm, D), lambda i, j: (i, 0)),
                  pl.BlockSpec((1, D), lambda i, j: (0, 0)),
                  pl.BlockSpec((D, tn), lambda i, j: (0, j))],
        out_specs=pl.BlockSpec((tm, tn), lambda i, j: (i, j)),
        out_shape=jax.ShapeDtypeStruct((T, N), F32),
        scratch_shapes=[pltpu.VMEM((tm, D), BF16)],
        compiler_params=_cparams(("parallel", "arbitrary")),
        name="norm_matmul",
    )(x, gain.reshape(1, D), w)


def _rope_kernel(zq_ref, kc_ref, ks_ref, kw_ref, cos_ref, sin_ref, q_o, kvc_o, kvs_o, kvw_o, *group_outs, kb):
    cos = cos_ref[...]
    sin = sin_ref[...]

    def rot(x):
        w = x.shape[1]
        n = w // LANES
        c = jnp.concatenate([cos] * n, axis=1) if n > 1 else cos
        s = jnp.concatenate([sin] * n, axis=1) if n > 1 else sin
        lane = lax.broadcasted_iota(jnp.int32, x.shape, 1)
        first = (lane % HEAD_DIM) < (HEAD_DIM // 2)
        sw = jnp.where(first, pltpu.roll(x, w - HEAD_DIM // 2, 1), pltpu.roll(x, HEAD_DIM // 2, 1))
        return x * c + sw * s

    kd = N_KV_HEADS * HEAD_DIM
    q_o[0] = (rot(zq_ref[0]) * (HEAD_DIM ** -0.5)).astype(BF16)
    outs = []
    for src, dst in ((kc_ref, kvc_o), (ks_ref, kvs_o), (kw_ref, kvw_o)):
        x = src[0]
        k = rot(x[:, :kd])
        v = x[:, kd:]
        dst[0] = jnp.concatenate([k, v], axis=1)
        outs.append((k, v))
    if group_outs:
        cpt_o, ksg_o, vst_o, kwg_o, vwt_o = group_outs
        tm = kc_ref.shape[1]
        kc_t, vc_t = outs[0][0].T, outs[0][1].T
        vs_t, vw_t = outs[1][1].T, outs[2][1].T
        for g in range(N_KV_HEADS):
            sl = slice(g * HEAD_DIM, (g + 1) * HEAD_DIM)
            ksg_o[0, g] = outs[1][0][:, sl].astype(BF16)
            kwg_o[0, g] = outs[2][0][:, sl].astype(BF16)
            for j in range(tm // kb):
                vst_o[0, g, j] = vs_t[sl, j * kb:(j + 1) * kb].astype(BF16)
                vwt_o[0, g, j] = vw_t[sl, j * kb:(j + 1) * kb].astype(BF16)
            for j in range(tm // PAGE):
                cpt_o[0, j, 0, g] = kc_t[sl, j * PAGE:(j + 1) * PAGE]
                cpt_o[0, j, 1, g] = vc_t[sl, j * PAGE:(j + 1) * PAGE]


def rope_split(z, cos, sin, tm, kb=None):
    B, S, _ = z.shape
    G, Dh = N_KV_HEADS, HEAD_DIM
    nt = S // tm
    zspec = lambda w, col: pl.BlockSpec((1, tm, w), lambda b, i: (b, i, col // w))
    tab = pl.BlockSpec((tm, LANES), lambda b, i: (i, 0))
    extra_specs, extra_shapes = [], []
    if kb is not None:
        kspec = pl.BlockSpec((1, G, tm, Dh), lambda b, i: (b, 0, i, 0))
        vspec = pl.BlockSpec((1, G, tm // kb, Dh, kb), lambda b, i: (b, 0, i, 0, 0))
        k_shape = jax.ShapeDtypeStruct((B, G, S, Dh), BF16)
        v_shape = jax.ShapeDtypeStruct((B, G, S // kb, Dh, kb), BF16)
        extra_specs = [pl.BlockSpec((1, tm // PAGE, 2, G, Dh, PAGE), lambda b, i: (b, i, 0, 0, 0, 0)),
                       kspec, vspec, kspec, vspec]
        extra_shapes = [jax.ShapeDtypeStruct((B, S // PAGE, 2, G, Dh, PAGE), F32),
                        k_shape, v_shape, k_shape, v_shape]
    return pl.pallas_call(
        functools.partial(_rope_kernel, kb=kb),
        grid=(B, nt),
        in_specs=[zspec(D_MODEL, COL_ZQ), zspec(KV_W, COL_KVC), zspec(KV_W, COL_KVS), zspec(KV_W, COL_KVW),
                  tab, tab],
        out_specs=[pl.BlockSpec((1, tm, D_MODEL), lambda b, i: (b, i, 0))]
                  + [pl.BlockSpec((1, tm, KV_W), lambda b, i: (b, i, 0))] * 3 + extra_specs,
        out_shape=[jax.ShapeDtypeStruct((B, S, D_MODEL), BF16)]
                  + [jax.ShapeDtypeStruct((B, S, KV_W), F32)] * 3 + extra_shapes,
        compiler_params=_cparams(("parallel", "parallel")),
        name="rope_split",
    )(z, z, z, z, cos, sin)


def rope_tables(pos):
    half = HEAD_DIM // 2
    inv = ROPE_THETA ** (-jnp.arange(half, dtype=F32) / half)
    ang = pos.astype(F32)[:, None] * inv[None, :]
    cos, sin = jnp.cos(ang), jnp.sin(ang)
    cos_t = jnp.concatenate([cos, cos] * (LANES // HEAD_DIM), axis=1)
    sin_t = jnp.concatenate([-sin, sin] * (LANES // HEAD_DIM), axis=1)
    return cos_t, sin_t


def _rg_gates(xr, wa_ref, ba_ref, wi_ref, bi_ref, lam_ref):
    xb = xr.astype(BF16)
    nblk = wa_ref.shape[0]
    bw = wa_ref.shape[1]
    ga = jnp.concatenate([_dot(xb[:, m * bw:(m + 1) * bw], wa_ref[m]) for m in range(nblk)], axis=1)
    gi = jnp.concatenate([_dot(xb[:, m * bw:(m + 1) * bw], wi_ref[m]) for m in range(nblk)], axis=1)
    r = jax.nn.sigmoid(ga + ba_ref[...])
    i = jax.nn.sigmoid(gi + bi_ref[...])
    nl = -lam_ref[...]
    softplus = jnp.maximum(nl, 0.0) + jnp.log1p(jnp.exp(-jnp.abs(nl)))
    log_a = -RG_C * r * softplus
    a = jnp.exp(log_a)
    bx = jnp.sqrt(-jnp.tanh(log_a) * (a * a + 1.0)) * (i * xr)
    return a, bx


def _rglru_kernel(zr_ref, zg_ref, cw_ref, cb_ref, wa_ref, ba_ref, wi_ref, bi_ref, lam_ref,
                  ya_o, h_o, xs, h_sc, a_sc, b_sc, hs_sc, *, tt):
    t = pl.program_id(1)

    @pl.when(t == 0)
    def _():
        xs[0:8, :] = jnp.zeros((8, xs.shape[1]), F32)
        h_sc[...] = jnp.zeros(h_sc.shape, F32)

    x = zr_ref[0]
    xs[8:8 + tt, :] = x
    w = cw_ref[...]
    nk = w.shape[0]
    xr = cb_ref[...] + w[nk - 1:nk] * x
    for k in range(nk - 1):
        sh = nk - 1 - k
        xr = xr + w[k:k + 1] * xs[8 - sh:8 - sh + tt, :]
    xs[0:8, :] = xs[tt:tt + 8, :]

    a, bx = _rg_gates(xr, wa_ref, ba_ref, wi_ref, bi_ref, lam_ref)
    a_sc[...] = a
    b_sc[...] = bx

    def body(i, h):
        r0 = pl.multiple_of(i * 8, 8)
        a8 = a_sc[pl.ds(r0, 8), :]
        b8 = b_sc[pl.ds(r0, 8), :]
        rows = []
        for k in range(8):
            h = a8[k:k + 1] * h + b8[k:k + 1]
            rows.append(h)
        hs_sc[pl.ds(r0, 8), :] = jnp.concatenate(rows, axis=0)
        return h

    h = lax.fori_loop(0, tt // 8, body, h_sc[0:1, :])
    h_sc[0:1, :] = h
    h_o[0] = h
    ya_o[0] = (jax.nn.gelu(zg_ref[0]) * hs_sc[...]).astype(BF16)


def rglru_prompt(z, cw, cb, wa, ba, wi, bi, lam, tt):
    B, S, _ = z.shape
    D = D_MODEL
    nblk, bw = wa.shape[0], wa.shape[1]
    vec = pl.BlockSpec((1, D), lambda b, t: (0, 0))
    wspec = pl.BlockSpec((nblk, bw, bw), lambda b, t: (0, 0, 0))
    return pl.pallas_call(
        functools.partial(_rglru_kernel, tt=tt),
        grid=(B, S // tt),
        in_specs=[pl.BlockSpec((1, tt, D), lambda b, t: (b, t, COL_ZR // D)),
                  pl.BlockSpec((1, tt, D), lambda b, t: (b, t, COL_ZG // D)),
                  pl.BlockSpec(cw.shape, lambda b, t: (0, 0)), vec, wspec, vec, wspec, vec, vec],
        out_specs=[pl.BlockSpec((1, tt, D), lambda b, t: (b, t, 0)),
                   pl.BlockSpec((1, 1, D), lambda b, t: (b, 0, 0))],
        out_shape=[jax.ShapeDtypeStruct((B, S, D), BF16), jax.ShapeDtypeStruct((B, 1, D), F32)],
        scratch_shapes=[pltpu.VMEM((tt + 8, D), F32), pltpu.VMEM((8, D), F32),
                        pltpu.VMEM((tt, D), F32), pltpu.VMEM((tt, D), F32), pltpu.VMEM((tt, D), F32)],
        compiler_params=_cparams(("parallel", "arbitrary")),
        name="rglru_prompt",
    )(z, z, cw, cb, wa, ba, wi, bi, lam)


CF_PAD = 32
CF_ROWS = 64


def _layernorm_silu(u, g, b):
    mu = jnp.mean(u, axis=-1, keepdims=True)
    d = u - mu
    var = jnp.mean(d * d, axis=-1, keepdims=True)
    y = d * lax.rsqrt(var + EPS) * g + b
    return y * jax.nn.sigmoid(y)


def _conformer_kernel(za_ref, zb_ref, w_ref, b_ref, g_ref, be_ref, yb_o, st_o, us, u_sc, *, tt):
    t = pl.program_id(1)
    D = us.shape[1]

    @pl.when(t == 0)
    def _():
        us[0:CF_PAD, :] = jnp.zeros((CF_PAD, D), F32)

    us[CF_PAD:CF_PAD + tt, :] = za_ref[0] * jax.nn.sigmoid(zb_ref[0])
    width = w_ref.shape[0]
    off = CF_PAD - (width - 1)
    for c in range(D // LANES):
        cs = slice(c * LANES, (c + 1) * LANES)
        for rs in range(tt // CF_ROWS):
            acc = jnp.broadcast_to(b_ref[:, cs], (CF_ROWS, LANES))
            for k in range(width):
                r0 = off + k + rs * CF_ROWS
                acc = acc + w_ref[k:k + 1, cs] * us[r0:r0 + CF_ROWS, cs]
            u_sc[rs * CF_ROWS:(rs + 1) * CF_ROWS, cs] = acc
    tail = us[tt:tt + CF_PAD, :]
    st_o[0] = tail
    us[0:CF_PAD, :] = tail
    yb_o[0] = _layernorm_silu(u_sc[...], g_ref[...], be_ref[...]).astype(BF16)


def conformer_prompt(z, w, b, g, be, tt):
    B, S, _ = z.shape
    D = D_MODEL
    vec = pl.BlockSpec((1, D), lambda bb, t: (0, 0))
    return pl.pallas_call(
        functools.partial(_conformer_kernel, tt=tt),
        grid=(B, S // tt),
        in_specs=[pl.BlockSpec((1, tt, D), lambda bb, t: (bb, t, COL_ZA // D)),
                  pl.BlockSpec((1, tt, D), lambda bb, t: (bb, t, COL_ZB // D)),
                  pl.BlockSpec(w.shape, lambda bb, t: (0, 0)), vec, vec, vec],
        out_specs=[pl.BlockSpec((1, tt, D), lambda bb, t: (bb, t, 0)),
                   pl.BlockSpec((1, CF_PAD, D), lambda bb, t: (bb, 0, 0))],
        out_shape=[jax.ShapeDtypeStruct((B, S, D), BF16), jax.ShapeDtypeStruct((B, CF_PAD, D), F32)],
        scratch_shapes=[pltpu.VMEM((tt + CF_PAD, D), F32), pltpu.VMEM((tt, D), F32)],
        compiler_params=_cparams(("parallel", "arbitrary")),
        name="conformer_prompt",
    )(z, z, w, b, g, be)


def _sample_mix_kernel(zr_ref, zg_ref, za_ref, zb_ref, rgst_ref, h0_ref, cfst_ref,
                       cw_ref, cb_ref, wa_ref, ba_ref, wi_ref, bi_ref, lam_ref,
                       fw_ref, fb_ref, fg_ref, fbe_ref,
                       ya_o, yb_o, h_o, u_o):
    x = zr_ref[...]
    w = cw_ref[...]
    nk = w.shape[0]
    xr = cb_ref[...] + w[nk - 1:nk] * x
    for k in range(nk - 1):
        xr = xr + w[k:k + 1] * rgst_ref[k]
    a, bx = _rg_gates(xr, wa_ref, ba_ref, wi_ref, bi_ref, lam_ref)
    h = a * h0_ref[...] + bx
    h_o[...] = h
    ya_o[...] = (jax.nn.gelu(zg_ref[...]) * h).astype(BF16)

    u_new = za_ref[...] * jax.nn.sigmoid(zb_ref[...])
    u_o[...] = u_new
    width = fw_ref.shape[0]
    acc = fb_ref[...] + fw_ref[width - 1:width, :] * u_new
    for k in range(width - 1):
        acc = acc + fw_ref[k:k + 1, :] * cfst_ref[k]
    yb_o[...] = _layernorm_silu(acc, fg_ref[...], fbe_ref[...]).astype(BF16)


def sample_mix(z, rgst, h0, cfst, cw, cb, wa, ba, wi, bi, lam, fw, fb, fg, fbe):
    DB = z.shape[0]
    D = D_MODEL
    full = lambda a: pl.BlockSpec(a.shape, lambda i: (0,) * a.ndim)
    zs = lambda col: pl.BlockSpec((DB, D), lambda i: (0, col // D))
    o = pl.BlockSpec((DB, D), lambda i: (0, 0))
    return pl.pallas_call(
        _sample_mix_kernel,
        grid=(1,),
        in_specs=[zs(COL_ZR), zs(COL_ZG), zs(COL_ZA), zs(COL_ZB), full(rgst), full(h0), full(cfst),
                  full(cw), full(cb), full(wa), full(ba), full(wi), full(bi), full(lam),
                  full(fw), full(fb), full(fg), full(fbe)],
        out_specs=[o, o, o, o],
        out_shape=[jax.ShapeDtypeStruct((DB, D), BF16), jax.ShapeDtypeStruct((DB, D), BF16),
                   jax.ShapeDtypeStruct((DB, D), F32), jax.ShapeDtypeStruct((DB, D), F32)],
        compiler_params=_cparams(("arbitrary",)),
        name="sample_mix",
    )(z, z, z, z, rgst, h0, cfst, cw, cb, wa, ba, wi, bi, lam, fw, fb, fg, fbe)


CMP_PAGES = 16
TILE_PITCH = 72


def _compress_kernel(pages_ref, *refs):
    del pages_ref
    G, Dh = N_KV_HEADS, HEAD_DIM
    x_refs = refs[:CMP_PAGES]
    wk_ref, wv_ref, ck_o, cv_o, kbuf, vbuf = refs[CMP_PAGES:]
    for p in range(CMP_PAGES):
        for g in range(G):
            r0 = (p * G + g) * TILE_PITCH
            kbuf[r0:r0 + Dh, :] = x_refs[p][0, g]
            vbuf[r0:r0 + Dh, :] = x_refs[p][1, g]
    m = CMP_PAGES * G

    ak = jnp.zeros((m, PAGE), F32)
    av = jnp.zeros((m, PAGE), F32)
    for d in range(Dh):
        lk = kbuf[pl.ds(d, m, stride=TILE_PITCH), :].astype(BF16)
        lv = vbuf[pl.ds(d, m, stride=TILE_PITCH), :].astype(BF16)
        ak = ak + _dot(lk, wk_ref[d])
        av = av + _dot(lv, wv_ref[d])
    ck_o[...] = ak
    cv_o[...] = av


def compress_pages(xt, layer, pages, wk, wv):
    n = pages.shape[0]
    G, Dh = N_KV_HEADS, HEAD_DIM
    specs = [pl.BlockSpec((None, None, 2, G, Dh, PAGE),
                          (lambda s, pg, i=i: (layer, pg[s * CMP_PAGES + i], 0, 0, 0, 0)))
             for i in range(CMP_PAGES)]
    wspec = pl.BlockSpec(wk.shape, lambda s, pg: (0, 0, 0))
    ospec = pl.BlockSpec((CMP_PAGES * G, PAGE), lambda s, pg: (s, 0))
    grid_spec = pltpu.PrefetchScalarGridSpec(
        num_scalar_prefetch=1,
        grid=(n // CMP_PAGES,),
        in_specs=specs + [wspec, wspec],
        out_specs=[ospec, ospec],
        scratch_shapes=[pltpu.VMEM((CMP_PAGES * G * TILE_PITCH, PAGE), F32)] * 2,
    )
    oshape = jax.ShapeDtypeStruct((n * G, PAGE), F32)
    return pl.pallas_call(
        _compress_kernel,
        grid_spec=grid_spec,
        out_shape=[oshape, oshape],
        compiler_params=_cparams(("arbitrary",)),
        name="compress_pages",
    )(pages, *([xt] * CMP_PAGES), wk, wv)


def _prep_cmp_weight(w):
    wt = jnp.transpose(w, (1, 0, 2))
    eye = jnp.eye(2, dtype=w.dtype)
    t = jnp.einsum('dje,hk->dhjke', wt, eye)
    return t.reshape(w.shape[1], 2 * w.shape[0], 2 * w.shape[2]).astype(BF16)


def _block_major(o, lead, n_pages):
    G, Dh = N_KV_HEADS, HEAD_DIM
    o = o.reshape(lead, n_pages, G, 2, Dh)
    return jnp.transpose(o, (0, 2, 1, 3, 4)).reshape(lead, G, 2 * n_pages, Dh)


ONES_ROWS = 16


def _softmax_rows(s):
    m = jnp.max(s, axis=-1, keepdims=True)
    e = jnp.exp(s - m)
    return e / jnp.sum(e, axis=-1, keepdims=True)


def _rank_select(v, n_sel):
    n = v.shape[0]
    iidx = lax.broadcasted_iota(jnp.int32, v.shape, 0)
    rank = jnp.zeros(v.shape, F32)
    for j in range(n):
        rj = v[j:j + 1]
        tie = jnp.where(iidx > j, 1.0, 0.0)
        rank = rank + jnp.where(rj > v, 1.0, jnp.where(rj == v, tie, 0.0))
    return jnp.where(rank < n_sel, 1.0, 0.0)


def _attn_prompt_kernel(q_ref, ck_ref, cvt_ref, ks_ref, vst_ref, kw_ref, vwt_ref, ng_ref, o_ref,
                        nsel_sc, s_sc, acc_sc, *, QB, KB):
    R, Dh = Q_PER_KV, HEAD_DIM
    NB = ck_ref.shape[2]
    pos0 = pl.program_id(2) * QB
    qblk = q_ref[0]
    qp = pos0 + lax.broadcasted_iota(jnp.int32, (1, QB), 1)
    blk = lax.broadcasted_iota(jnp.int32, (NB, 1), 0)

    cmask = ((blk + 1) * CMP_BLOCK - 1) <= qp
    cmf = jnp.where(cmask, 1.0, 0.0)
    ck, cvt = ck_ref[0, 0], cvt_ref[0, 0]
    imp = jnp.zeros((NB, QB), F32)
    o_cmp = []
    for r in range(R):
        s = jnp.where(cmask, _dot_nt(ck, qblk[:, r * Dh:(r + 1) * Dh]), NEG_INF)
        e = jnp.exp(s - jnp.max(s, axis=0, keepdims=True))
        p = e / jnp.sum(e, axis=0, keepdims=True) * cmf
        o_cmp.append(_dot(cvt, p.astype(BF16)))
        imp = imp + p
    imp = jnp.where(blk == qp // CMP_BLOCK, FORCE_LOCAL,
                    jnp.where(blk == 0, FORCE_INIT, jnp.where(blk * CMP_BLOCK <= qp, imp, -1.0)))
    nsel_sc[...] = jnp.where(_rank_select(imp, N_SEL) > 0.5, 0.0, NEG_INF)

    ones = jnp.ones((ONES_ROWS, KB), BF16)
    qs = [qblk[:, r * Dh:(r + 1) * Dh] for r in range(R)]
    bpt = KB // CMP_BLOCK
    n_t = (pos0 + QB) // KB

    def key_col(kt):
        return kt * KB + lax.broadcasted_iota(jnp.int32, (KB, 1), 0)

    def tile_max(s):
        return jnp.max(s.reshape(KB // 8, 8, QB), axis=0)

    def normalised(acc):
        return acc[:Dh] / acc[Dh:Dh + 1]

    def score_tile(kt, mx):
        k0 = pl.multiple_of(kt * KB, KB)
        k_t = ks_ref[0, 0, pl.ds(k0, KB), :]
        bias = jnp.concatenate([jnp.broadcast_to(nsel_sc[pl.ds(kt * bpt + i, 1), :], (CMP_BLOCK, QB))
                                for i in range(bpt)], axis=0)
        bias = bias + jnp.where(key_col(kt) <= qp, 0.0, NEG_INF)
        out = []
        for r in range(R):
            s = _dot_nt(k_t, qs[r]) + bias
            s_sc[r, kt] = s
            out.append(jnp.maximum(mx[r], tile_max(s)))
        return tuple(out)

    mx = lax.fori_loop(0, n_t, score_tile, tuple(jnp.full((8, QB), NEG_INF, F32) for _ in range(R)))
    m_sel = [jnp.max(x, axis=0, keepdims=True) for x in mx]

    acc_sc[...] = jnp.zeros(acc_sc.shape, F32)

    def value_tile(kt, carry):
        v1 = jnp.concatenate([vst_ref[0, 0, kt], ones], axis=0)
        for r in range(R):
            p = jnp.exp(s_sc[r, kt] - m_sel[r]).astype(BF16)
            acc_sc[r] += _dot(v1, p)
        return carry

    lax.fori_loop(0, n_t, value_tile, 0)

    kt_hi = n_t - 1
    w_tiles = []
    for i in range(WINDOW // KB + QB // KB):
        kt = kt_hi - i
        ktc = jnp.maximum(kt, 0)
        key = key_col(kt)
        d = qp - key
        bias = jnp.where(d >= 0, jnp.where(d <= WINDOW, jnp.where(key >= 0, 0.0, NEG_INF), NEG_INF), NEG_INF)
        k_t = kw_ref[0, 0, pl.ds(pl.multiple_of(ktc * KB, KB), KB), :]
        w_tiles.append((k_t, jnp.concatenate([vwt_ref[0, 0, ktc], ones], axis=0), bias))

    gt = jax.nn.sigmoid(ng_ref[0]).T
    ys = []
    for r in range(R):
        s_w = [_dot_nt(k_t, qs[r]) + bias for k_t, _, bias in w_tiles]
        mw = tile_max(s_w[0])
        for s in s_w[1:]:
            mw = jnp.maximum(mw, tile_max(s))
        mw = jnp.max(mw, axis=0, keepdims=True)
        acc_w = _dot(w_tiles[0][1], jnp.exp(s_w[0] - mw).astype(BF16))
        for (_, v1, _), s in zip(w_tiles[1:], s_w[1:]):
            acc_w = acc_w + _dot(v1, jnp.exp(s - mw).astype(BF16))
        ys.append(gt[3 * r:3 * r + 1] * o_cmp[r] + gt[3 * r + 1:3 * r + 2] * normalised(acc_sc[r])
                  + gt[3 * r + 2:3 * r + 3] * normalised(acc_w))
    o_ref[0] = jnp.concatenate(ys, axis=0).T.astype(BF16)


def attn_prompt(q, ck, cvt, ksg, vst, kwg, vwt, z, QB, KB):
    B, S, _ = q.shape
    G, R, Dh = N_KV_HEADS, Q_PER_KV, HEAD_DIM
    NB = ck.shape[2]
    kspec = pl.BlockSpec((1, 1, S, Dh), lambda b, g, c: (b, g, 0, 0))
    vspec = pl.BlockSpec((1, 1, S // KB, Dh, KB), lambda b, g, c: (b, g, 0, 0, 0))
    return pl.pallas_call(
        functools.partial(_attn_prompt_kernel, QB=QB, KB=KB),
        grid=(B, G, S // QB),
        in_specs=[pl.BlockSpec((1, QB, R * Dh), lambda b, g, c: (b, c, g)),
                  pl.BlockSpec((1, 1, NB, Dh), lambda b, g, c: (b, g, 0, 0)),
                  pl.BlockSpec((1, 1, Dh, NB), lambda b, g, c: (b, g, 0, 0)),
                  kspec, vspec, kspec, vspec,
                  pl.BlockSpec((1, QB, LANES), lambda b, g, c: (b, c, COL_NG // LANES + g))],
        out_specs=pl.BlockSpec((1, QB, R * Dh), lambda b, g, c: (b, c, g)),
        out_shape=jax.ShapeDtypeStruct((B, S, G * R * Dh), BF16),
        scratch_shapes=[pltpu.VMEM((NB, QB), F32), pltpu.VMEM((R, S // KB, KB, QB), F32),
                        pltpu.VMEM((R, Dh + ONES_ROWS, QB), F32)],
        compiler_params=_cparams(("parallel", "parallel", "arbitrary")),
        name="attn_prompt",
    )(q, ck, cvt, ksg, vst, kwg, vwt, z)


QPAD = 16


def _pad_rows(x, n):
    return jnp.concatenate([x, jnp.zeros((n - x.shape[0], x.shape[1]), x.dtype)], axis=0)


def _sample_cmp_kernel(q_ref, ck_ref, cv_ref, o_ref, idx_ref, *, NBC):
    G, R = N_KV_HEADS, Q_PER_KV
    imps = []
    for g in range(G):
        q = _pad_rows(q_ref[g], QPAD).astype(BF16)
        p = _softmax_rows(_dot_nt(q, ck_ref[g]))
        o_ref[g] = _dot(p.astype(BF16), cv_ref[g])[:R]
        imps.append(p[0:1] + p[1:2] + p[2:3] + p[3:4])
    v = jnp.concatenate(imps + [jnp.full((8 - G, NBC), -1.0, F32)], axis=0)
    lane = lax.broadcasted_iota(jnp.int32, v.shape, 1)
    v = jnp.where(lane == 0, -1.0, v)
    out_lane = lax.broadcasted_iota(jnp.int32, (8, LANES), 1)
    idx = jnp.where(out_lane == N_SEL - 1, NBC, 0)
    for t in range(1, N_SEL - 1):
        m = jnp.max(v, axis=-1, keepdims=True)
        first = jnp.min(jnp.where(v == m, lane, NBC), axis=-1, keepdims=True)
        idx = jnp.where(out_lane == t, first, idx)
        v = jnp.where(lane == first, -2.0, v)
    idx_ref[...] = idx


def sample_cmp(q, ck, cv):
    DB, G, R, Dh = q.shape
    NBC = ck.shape[2]
    return pl.pallas_call(
        functools.partial(_sample_cmp_kernel, NBC=NBC),
        grid=(DB,),
        in_specs=[pl.BlockSpec((None, G, R, Dh), lambda b: (b, 0, 0, 0)),
                  pl.BlockSpec((None, G, NBC, Dh), lambda b: (b, 0, 0, 0)),
                  pl.BlockSpec((None, G, NBC, Dh), lambda b: (b, 0, 0, 0))],
        out_specs=[pl.BlockSpec((None, G, R, Dh), lambda b: (b, 0, 0, 0)),
                   pl.BlockSpec((None, 8, LANES), lambda b: (b, 0, 0))],
        out_shape=[jax.ShapeDtypeStruct((DB, G, R, Dh), F32), jax.ShapeDtypeStruct((DB, 8, LANES), jnp.int32)],
        compiler_params=_cparams(("parallel",)),
        name="sample_cmp",
    )(q, ck, cv)


N_GATHER = N_SEL - 1


def _sample_attn_kernel(idx_ref, pt_ref, *refs):
    del pt_ref
    R, Dh = Q_PER_KV, HEAD_DIM
    blk_refs = refs[:N_GATHER]
    q_ref, ns_ref, win_ref, nw_ref, oc_ref, ng_ref, o_ref = refs[N_GATHER:]
    base = (pl.program_id(0) * pl.num_programs(1) + pl.program_id(1)) * N_SEL
    q = _pad_rows(q_ref[...], QPAD).astype(BF16)
    lane = lax.broadcasted_iota(jnp.int32, (1, LANES), 1)
    row0 = lax.broadcasted_iota(jnp.int32, (LANES, Dh), 0) == 0

    def new_tile(n_ref, kv):
        return jnp.where(row0, jnp.broadcast_to(n_ref[kv], (LANES, Dh)), 0.0).astype(BF16)

    def new_scores(n_ref):
        return jnp.where(lane == 0, _dot_nt(q, new_tile(n_ref, 0)), NEG_INF)

    tiles = []
    for k in range(N_GATHER):
        half = idx_ref[base + k] % 2
        s_k = _dot(q, blk_refs[k][0].astype(BF16))
        tiles.append(jnp.where(lane // CMP_BLOCK == half, s_k, NEG_INF))
    p = _softmax_rows(jnp.concatenate(tiles + [new_scores(ns_ref)], axis=1)).astype(BF16)
    o = _dot(p[:, N_GATHER * LANES:], new_tile(ns_ref, 1))
    for k in range(N_GATHER):
        o = o + _dot_nt(p[:, k * LANES:(k + 1) * LANES], blk_refs[k][1].astype(BF16))
    o_slc = o[:R]

    W = win_ref.shape[2]
    p = _softmax_rows(jnp.concatenate([_dot(q, win_ref[0].astype(BF16)), new_scores(nw_ref)], axis=1)).astype(BF16)
    o_win = (_dot_nt(p[:, :W], win_ref[1].astype(BF16)) + _dot(p[:, W:], new_tile(nw_ref, 1)))[:R]

    gates = jax.nn.sigmoid(ng_ref[...])
    o_cmp = oc_ref[...]
    rows = []
    for r in range(R):
        rows.append(gates[:, 3 * r:3 * r + 1] * o_cmp[r:r + 1] + gates[:, 3 * r + 1:3 * r + 2] * o_slc[r:r + 1]
                    + gates[:, 3 * r + 2:3 * r + 3] * o_win[r:r + 1])
    o_ref[...] = jnp.concatenate(rows, axis=0)


def sample_attn(idx, page_table, layer, slc_t, q, new_s, win_t, new_w, o_cmp, ng):
    DB, G, R, Dh = q.shape
    npg = page_table.shape[1]
    W = win_t.shape[5]
    pt = page_table.reshape(-1)

    def blk_map(k):
        def im(b, g, idx_r, pt_r):
            n = idx_r[(b * G + g) * N_SEL + k]
            return (layer, pt_r[b * npg + n // 2], 0, g, 0, 0)
        return im

    bg = lambda *tail: (lambda b, g, i, p: (b, g) + tail)
    grid_spec = pltpu.PrefetchScalarGridSpec(
        num_scalar_prefetch=2,
        grid=(DB, G),
        in_specs=[pl.BlockSpec((None, None, 2, None, Dh, PAGE), blk_map(k)) for k in range(N_GATHER)]
                 + [pl.BlockSpec((None, None, R, Dh), bg(0, 0)),
                    pl.BlockSpec((None, None, 2, 1, Dh), bg(0, 0, 0)),
                    pl.BlockSpec((None, None, 2, None, Dh, W), lambda b, g, i, p: (layer, b, 0, g, 0, 0)),
                    pl.BlockSpec((None, None, 2, 1, Dh), bg(0, 0, 0)),
                    pl.BlockSpec((None, None, R, Dh), bg(0, 0)),
                    pl.BlockSpec((None, None, 1, LANES), bg(0, 0))],
        out_specs=pl.BlockSpec((None, None, R, Dh), bg(0, 0)),
    )
    return pl.pallas_call(
        _sample_attn_kernel,
        grid_spec=grid_spec,
        out_shape=jax.ShapeDtypeStruct((DB, G, R, Dh), F32),
        compiler_params=_cparams(("arbitrary", "arbitrary")),
        name="sample_attn",
    )(idx, pt, *([slc_t] * N_GATHER), q, new_s, win_t, new_w, o_cmp, ng)


def _merge_kernel(x_ref, ya_ref, yb_ref, yc_ref, m0_ref, m1_ref, m2_ref, wa_ref, wb_ref, wc_ref, wo_ref, o_ref):
    m = (jax.nn.sigmoid(m0_ref[...]) * _dot(ya_ref[...], wa_ref[...])
         + jax.nn.sigmoid(m1_ref[...]) * _dot(yb_ref[...], wb_ref[...])
         + jax.nn.sigmoid(m2_ref[...]) * _dot(yc_ref[...], wc_ref[...]))
    o_ref[...] = x_ref[...] + _dot(m.astype(BF16), wo_ref[...])


def merge_out(x, ya, yb, yc, z, wa, wb, wc, wo, tm):
    T, D = x.shape
    row = pl.BlockSpec((tm, D), lambda i: (i, 0))
    zs = lambda k: pl.BlockSpec((tm, D), lambda i: (i, COL_MG // D + k))
    wsp = pl.BlockSpec((D, D), lambda i: (0, 0))
    return pl.pallas_call(
        _merge_kernel,
        grid=(T // tm,),
        in_specs=[row, row, row, row, zs(0), zs(1), zs(2), wsp, wsp, wsp, wsp],
        out_specs=row,
        out_shape=jax.ShapeDtypeStruct((T, D), F32),
        compiler_params=_cparams(("parallel",)),
        name="merge_out",
    )(x, ya, yb, yc, z, z, z, wa, wb, wc, wo)


def _ffn_kernel(x_ref, g_ref, wg_ref, wu_ref, wd_ref, o_ref, hn_sc, acc_sc):
    j = pl.program_id(1)

    @pl.when(j == 0)
    def _():
        x = x_ref[...]
        ms = jnp.mean(x * x, axis=-1, keepdims=True)
        hn_sc[...] = (x * lax.rsqrt(ms + EPS) * g_ref[...]).astype(BF16)
        acc_sc[...] = jnp.zeros(acc_sc.shape, F32)

    hn = hn_sc[...]
    fg = _dot(hn, wg_ref[...])
    fu = _dot(hn, wu_ref[...])
    act = (fg * jax.nn.sigmoid(fg)) * fu
    acc_sc[...] += _dot(act.astype(BF16), wd_ref[...])

    @pl.when(j == pl.num_programs(1) - 1)
    def _():
        o_ref[...] = x_ref[...] + acc_sc[...]


def ffn(x, gain, w_gu, w_down, tm, fc):
    T, D = x.shape
    F = w_down.shape[0]
    nf = F // fc
    return pl.pallas_call(
        _ffn_kernel,
        grid=(T // tm, nf),
        in_specs=[pl.BlockSpec((tm, D), lambda i, j: (i, 0)),
                  pl.BlockSpec((1, D), lambda i, j: (0, 0)),
                  pl.BlockSpec((D, fc), lambda i, j: (0, j)),
                  pl.BlockSpec((D, fc), lambda i, j: (0, nf + j)),
                  pl.BlockSpec((fc, D), lambda i, j: (j, 0))],
        out_specs=pl.BlockSpec((tm, D), lambda i, j: (i, 0)),
        out_shape=jax.ShapeDtypeStruct((T, D), F32),
        scratch_shapes=[pltpu.VMEM((tm, D), BF16), pltpu.VMEM((tm, D), F32)],
        compiler_params=_cparams(("parallel", "arbitrary")),
        name="ffn",
    )(x, gain.reshape(1, D), w_gu, w_gu, w_down)


def _rmsnorm_kernel(x_ref, g_ref, o_ref):
    x = x_ref[...]
    ms = jnp.mean(x * x, axis=-1, keepdims=True)
    o_ref[...] = x * lax.rsqrt(ms + EPS) * g_ref[...]


def rmsnorm_rows(x, gain, tm):
    T, D = x.shape
    return pl.pallas_call(
        _rmsnorm_kernel,
        grid=(T // tm,),
        in_specs=[pl.BlockSpec((tm, D), lambda i: (i, 0)), pl.BlockSpec((1, D), lambda i: (0, 0))],
        out_specs=pl.BlockSpec((tm, D), lambda i: (i, 0)),
        out_shape=jax.ShapeDtypeStruct((T, D), F32),
        compiler_params=_cparams(("parallel",)),
        name="final_rmsnorm",
    )(x, gain.reshape(1, D))


def _prep_w_in(w):
    D = w.shape[0]
    G, R = N_KV_HEADS, Q_PER_KV
    n_gate = 3 * G * R
    head, gates, tail = w[:, :COL_NG], w[:, COL_NG:COL_NG + n_gate], w[:, COL_NG + n_gate:]
    gates = gates.reshape(D, G, 3 * R)
    gates = jnp.pad(gates, ((0, 0), (0, 0), (0, LANES - 3 * R))).reshape(D, G * LANES)
    return jnp.concatenate([head, gates, tail], axis=1).astype(BF16)


RG_TILE = 256


def _prep_blockdiag(w):
    n, c, _ = w.shape
    per = RG_TILE // c
    w = w.reshape(n // per, per, c, c)
    eye = jnp.eye(per, dtype=w.dtype)
    t = jnp.einsum('mpcd,pq->mpcqd', w, eye)
    return t.reshape(n // per, RG_TILE, RG_TILE).astype(BF16)


def _tile(n, pref):
    t = min(n, pref)
    assert n % t == 0, (n, t)
    return t


def _token_minor(cache):
    nd = cache.ndim
    return jnp.transpose(cache, tuple(range(nd - 4)) + (nd - 3, nd - 2, nd - 1, nd - 4))


def kernel(x_prompt, x_sample, cache_cmp_kv, cache_slc_kv, cache_win_kv, state_rglru_conv, state_rglru_h,
           state_conf_conv, page_table, norm1_g, w_in, rg_conv_w, rg_conv_b, rg_wa, rg_ba, rg_wi, rg_bi,
           rg_lambda, cf_dw_w, cf_dw_b, cf_ln_g, cf_ln_b, nsa_w_ck, nsa_w_cv, w_br_rnn, w_br_conv, w_br_attn,
           w_out, norm2_g, w_ffn_gu, w_ffn_down, final_norm_g):
    B, S, D = x_prompt.shape
    DB = x_sample.shape[0]
    depth = w_in.shape[0]
    G, R, Dh = N_KV_HEADS, Q_PER_KV, HEAD_DIM
    n_pages = page_table.shape[1]
    past_len = n_pages * PAGE
    w_buf = cache_win_kv.shape[2]
    d_ff = w_ffn_down.shape[1]
    rg_k = rg_conv_w.shape[1]
    cf_k = cf_dw_w.shape[1]
    assert D == D_MODEL and x_sample.shape[1] == 1 and cache_cmp_kv.shape[2] == PAGE
    assert cache_cmp_kv.shape[3:] == (2, G, Dh) and w_buf == min(WINDOW, past_len)
    assert S % PAGE == 0 and cf_k - 1 <= CF_PAD

    T = B * S
    tm_p = _tile(T, 1024)
    tm_rope = _tile(S, 512)
    tt = _tile(S, 256)
    QB = _tile(S, 256)
    KB = _tile(S, 256)
    assert QB % KB == 0 and WINDOW % KB == 0 and KB % CMP_BLOCK == 0
    fc = d_ff // 2 if (d_ff // 2) % LANES == 0 else d_ff

    cos_p, sin_p = rope_tables(jnp.arange(S, dtype=jnp.int32))
    cos_s, sin_s = rope_tables(jnp.full((DB,), past_len, jnp.int32))
    prompt_pages = jnp.arange(T // PAGE, dtype=jnp.int32)
    sample_pages = page_table.reshape(-1).astype(jnp.int32)
    cmp_t, slc_t, win_t = _token_minor(cache_cmp_kv), _token_minor(cache_slc_kv), _token_minor(cache_win_kv)
    row2 = lambda v: v.reshape(1, -1)

    xp = x_prompt.reshape(T, D)
    xs = x_sample.reshape(DB, D)
    st_p, st_s = [], []
    for l in range(depth):
        w_in_l = _prep_w_in(w_in[l])
        wa_l, wi_l = _prep_blockdiag(rg_wa[l]), _prep_blockdiag(rg_wi[l])
        wk_c, wv_c = _prep_cmp_weight(nsa_w_ck[l]), _prep_cmp_weight(nsa_w_cv[l])
        w_a, w_b, w_c = w_br_rnn[l].astype(BF16), w_br_conv[l].astype(BF16), w_br_attn[l].astype(BF16)
        w_o, w_gu, w_dn = w_out[l].astype(BF16), w_ffn_gu[l].astype(BF16), w_ffn_down[l].astype(BF16)
        rg_args = (rg_conv_w[l], row2(rg_conv_b[l]), wa_l, row2(rg_ba[l]), wi_l, row2(rg_bi[l]),
                   row2(rg_lambda[l]))
        cf_args = (cf_dw_w[l], row2(cf_dw_b[l]), row2(cf_ln_g[l]), row2(cf_ln_b[l]))

        z = norm_matmul(xp, norm1_g[l], w_in_l, tm_p, 1024).reshape(B, S, N_PAD)
        ya, h_last = rglru_prompt(z, *rg_args, tt)
        yb, cf_tail = conformer_prompt(z, *cf_args, tt)
        q, kvc, kvs, kvw, kvc_t, ksg, vst, kwg, vwt = rope_split(z, cos_p, sin_p, tm_rope, KB)
        ck, cv = compress_pages(kvc_t.reshape(1, T // PAGE, 2, G, Dh, PAGE), 0, prompt_pages, wk_c, wv_c)
        ck = _block_major(ck, B, S // PAGE).astype(BF16)
        cvt = jnp.transpose(_block_major(cv, B, S // PAGE), (0, 1, 3, 2)).astype(BF16)
        yc = attn_prompt(q, ck, cvt, ksg, vst, kwg, vwt, z, QB, KB)
        xp = merge_out(xp, ya.reshape(T, D), yb.reshape(T, D), yc.reshape(T, D), z.reshape(T, N_PAD),
                       w_a, w_b, w_c, w_o, _tile(T, 512))
        xp = ffn(xp, norm2_g[l], w_gu, w_dn, _tile(T, 512), fc)
        kv6 = lambda a: a.reshape(B, S, 2, G, Dh)
        st_p.append((kv6(kvc), kv6(kvs), kv6(kvw)[:, S - w_buf:] if S >= w_buf else
                     jnp.pad(kv6(kvw), ((0, 0), (w_buf - S, 0), (0, 0), (0, 0), (0, 0))),
                     z[:, S - (rg_k - 1):, COL_ZR:COL_ZR + D], h_last.reshape(B, D),
                     cf_tail[:, CF_PAD - (cf_k - 1):]))

        zs = norm_matmul(xs, norm1_g[l], w_in_l, DB, 1024)
        rgst = jnp.transpose(state_rglru_conv[l], (1, 0, 2))
        cfst = jnp.transpose(state_conf_conv[l], (1, 0, 2))
        ya_s, yb_s, h_s, u_s = sample_mix(zs, rgst, state_rglru_h[l], cfst, *rg_args, *cf_args)
        q_s, kvc_s, kvs_s, kvw_s = rope_split(zs.reshape(1, DB, N_PAD), cos_s, sin_s, DB)
        q_s = q_s.reshape(DB, G, R, Dh).astype(F32)
        ck_s, cv_s = compress_pages(cmp_t, l, sample_pages, wk_c, wv_c)
        o_cmp_s, idx = sample_cmp(q_s, _block_major(ck_s, DB, n_pages).astype(BF16),
                                  _block_major(cv_s, DB, n_pages).astype(BF16))
        idx = idx[:, :G, :N_SEL].reshape(-1)
        new_rows = lambda a: jnp.transpose(a.reshape(DB, 2, G, 1, Dh), (0, 2, 1, 3, 4))
        ng_s = zs[:, COL_NG:COL_NG + G * LANES].reshape(DB, G, 1, LANES)
        yc_s = sample_attn(idx, page_table.astype(jnp.int32), l, slc_t, q_s, new_rows(kvs_s[0]), win_t,
                           new_rows(kvw_s[0]), o_cmp_s, ng_s)
        xs = merge_out(xs, ya_s, yb_s, yc_s.reshape(DB, D).astype(BF16), zs, w_a, w_b, w_c, w_o, DB)
        xs = ffn(xs, norm2_g[l], w_gu, w_dn, DB, fc)
        kv6s = lambda a: a.reshape(DB, 1, 2, G, Dh)
        st_s.append((kv6s(kvc_s), kv6s(kvs_s),
                     jnp.concatenate([cache_win_kv[l], kv6s(kvw_s)], axis=1)[:, 1:],
                     jnp.concatenate([state_rglru_conv[l], zs[:, None, COL_ZR:COL_ZR + D]], axis=1)[:, 1:],
                     h_s,
                     jnp.concatenate([state_conf_conv[l], u_s[:, None]], axis=1)[:, 1:]))

    y_prompt = rmsnorm_rows(xp, final_norm_g, tm_p).reshape(B, S, D)
    y_sample = rmsnorm_rows(xs, final_norm_g, DB).reshape(DB, 1, D)
    stk = lambda sts, j: jnp.stack([st[j] for st in sts], axis=0)
    return (y_prompt, y_sample,
            stk(st_p, 0), stk(st_p, 1), stk(st_p, 2), stk(st_p, 3), stk(st_p, 4), stk(st_p, 5),
            stk(st_s, 0), stk(st_s, 1), stk(st_s, 2), stk(st_s, 3), stk(st_s, 4), stk(st_s, 5))
```

```python
import functools

import jax
import jax.numpy as jnp
from jax import lax
from jax.experimental import pallas as pl
from jax.experimental.pallas import tpu as pltpu

F32 = jnp.float32
BF16 = jnp.bfloat16

CMP_BLOCK = 64
N_SEL = 16
WINDOW = 512
FORCE_LOCAL = 1e4
FORCE_INIT = 5e3
RG_C = 8.0
ROPE_THETA = 10000.0
EPS = 1e-6
NEG_INF = -1e30

HEAD_DIM = 64
N_KV_HEADS = 4
Q_PER_KV = 4
LANES = 128
VMEM_LIMIT = 56 * 1024 * 1024

D_MODEL = 1024
COL_ZR, COL_ZG, COL_ZA, COL_ZB, COL_ZQ = 0, 1024, 2048, 3072, 4096
COL_KVC, COL_KVS, COL_KVW = 5120, 5632, 6144
COL_NG = 6656
COL_MG = 7168
N_PAD = COL_MG + 3 * D_MODEL
KV_W = 2 * N_KV_HEADS * HEAD_DIM
PAGE = 2 * CMP_BLOCK


def _cparams(sem):
    return pltpu.CompilerParams(dimension_semantics=sem, vmem_limit_bytes=VMEM_LIMIT)


def _dot(a, b):
    return jnp.dot(a, b, preferred_element_type=F32)


def _dot_nt(a, b):
    return lax.dot_general(a, b, (((1,), (1,)), ((), ())), preferred_element_type=F32)


def _norm_mm_kernel(x_ref, g_ref, w_ref, o_ref, hn_sc):
    @pl.when(pl.program_id(1) == 0)
    def _():
        x = x_ref[...]
        ms = jnp.mean(x * x, axis=-1, keepdims=True)
        hn_sc[...] = (x * lax.rsqrt(ms + EPS) * g_ref[...]).astype(BF16)

    o_ref[...] = _dot(hn_sc[...], w_ref[...])


def norm_matmul(x, gain, w, tm, tn):
    T, D = x.shape
    N = w.shape[1]
    return pl.pallas_call(
        _norm_mm_kernel,
        grid=(T // tm, N // tn),
        in_specs=[pl.BlockSpec((tm, D), lambda i, j: (i, 0)),
                  pl.BlockSpec((1, D), lambda i, j: (0, 0)),
                  pl.BlockSpec((D, tn), lambda i, j: (0, j))],
        out_specs=pl.BlockSpec((tm, tn), lambda i, j: (i, j)),
        out_shape=jax.ShapeDtypeStruct((T, N), F32),
        scratch_shapes=[pltpu.VMEM((tm, D), BF16)],
        compiler_params=_cparams(("parallel", "arbitrary")),
        name="norm_matmul",
    )(x, gain.reshape(1, D), w)


def _rope_kernel(zq_ref, kc_ref, ks_ref, kw_ref, cos_ref, sin_ref, q_o, kvc_o, kvs_o, kvw_o, *group_outs, kb):
    cos = cos_ref[...]
    sin = sin_ref[...]

    def rot(x):
        w = x.shape[1]
        n = w // LANES
        c = jnp.concatenate([cos] * n, axis=1) if n > 1 else cos
        s = jnp.concatenate([sin] * n, axis=1) if n > 1 else sin
        lane = lax.broadcasted_iota(jnp.int32, x.shape, 1)
        first = (lane % HEAD_DIM) < (HEAD_DIM // 2)
        sw = jnp.where(first, pltpu.roll(x, w - HEAD_DIM // 2, 1), pltpu.roll(x, HEAD_DIM // 2, 1))
        return x * c + sw * s

    kd = N_KV_HEADS * HEAD_DIM
    q_o[0] = (rot(zq_ref[0]) * (HEAD_DIM ** -0.5)).astype(BF16)
    outs = []
    for src, dst in ((kc_ref, kvc_o), (ks_ref, kvs_o), (kw_ref, kvw_o)):
        x = src[0]
        k = rot(x[:, :kd])
        v = x[:, kd:]
        dst[0] = jnp.concatenate([k, v], axis=1)
        outs.append((k, v))
    if group_outs:
        cpt_o, ksg_o, vst_o, kwg_o, vwt_o = group_outs
        tm = kc_ref.shape[1]
        kc_t, vc_t = outs[0][0].T, outs[0][1].T
        vs_t, vw_t = outs[1][1].T, outs[2][1].T
        for g in range(N_KV_HEADS):
            sl = slice(g * HEAD_DIM, (g + 1) * HEAD_DIM)
            ksg_o[0, g] = outs[1][0][:, sl].astype(BF16)
            kwg_o[0, g] = outs[2][0][:, sl].astype(BF16)
            for j in range(tm // kb):
                vst_o[0, g, j] = vs_t[sl, j * kb:(j + 1) * kb].astype(BF16)
                vwt_o[0, g, j] = vw_t[sl, j * kb:(j + 1) * kb].astype(BF16)
            for j in range(tm // PAGE):
                cpt_o[0, j, 0, g] = kc_t[sl, j * PAGE:(j + 1) * PAGE]
                cpt_o[0, j, 1, g] = vc_t[sl, j * PAGE:(j + 1) * PAGE]


def rope_split(z, cos, sin, tm, kb=None):
    B, S, _ = z.shape
    G, Dh = N_KV_HEADS, HEAD_DIM
    nt = S // tm
    zspec = lambda w, col: pl.BlockSpec((1, tm, w), lambda b, i: (b, i, col // w))
    tab = pl.BlockSpec((tm, LANES), lambda b, i: (i, 0))
    extra_specs, extra_shapes = [], []
    if kb is not None:
        kspec = pl.BlockSpec((1, G, tm, Dh), lambda b, i: (b, 0, i, 0))
        vspec = pl.BlockSpec((1, G, tm // kb, Dh, kb), lambda b, i: (b, 0, i, 0, 0))
        k_shape = jax.ShapeDtypeStruct((B, G, S, Dh), BF16)
        v_shape = jax.ShapeDtypeStruct((B, G, S // kb, Dh, kb), BF16)
        extra_specs = [pl.BlockSpec((1, tm // PAGE, 2, G, Dh, PAGE), lambda b, i: (b, i, 0, 0, 0, 0)),
                       kspec, vspec, kspec, vspec]
        extra_shapes = [jax.ShapeDtypeStruct((B, S // PAGE, 2, G, Dh, PAGE), F32),
                        k_shape, v_shape, k_shape, v_shape]
    return pl.pallas_call(
        functools.partial(_rope_kernel, kb=kb),
        grid=(B, nt),
        in_specs=[zspec(D_MODEL, COL_ZQ), zspec(KV_W, COL_KVC), zspec(KV_W, COL_KVS), zspec(KV_W, COL_KVW),
                  tab, tab],
        out_specs=[pl.BlockSpec((1, tm, D_MODEL), lambda b, i: (b, i, 0))]
                  + [pl.BlockSpec((1, tm, KV_W), lambda b, i: (b, i, 0))] * 3 + extra_specs,
        out_shape=[jax.ShapeDtypeStruct((B, S, D_MODEL), BF16)]
                  + [jax.ShapeDtypeStruct((B, S, KV_W), F32)] * 3 + extra_shapes,
        compiler_params=_cparams(("parallel", "parallel")),
        name="rope_split",
    )(z, z, z, z, cos, sin)


def rope_tables(pos):
    half = HEAD_DIM // 2
    inv = ROPE_THETA ** (-jnp.arange(half, dtype=F32) / half)
    ang = pos.astype(F32)[:, None] * inv[None, :]
    cos, sin = jnp.cos(ang), jnp.sin(ang)
    cos_t = jnp.concatenate([cos, cos] * (LANES // HEAD_DIM), axis=1)
    sin_t = jnp.concatenate([-sin, sin] * (LANES // HEAD_DIM), axis=1)
    return cos_t, sin_t


def _rg_gates(xr, wa_ref, ba_ref, wi_ref, bi_ref, lam_ref):
    xb = xr.astype(BF16)
    nblk = wa_ref.shape[0]
    bw = wa_ref.shape[1]
    ga = jnp.concatenate([_dot(xb[:, m * bw:(m + 1) * bw], wa_ref[m]) for m in range(nblk)], axis=1)
    gi = jnp.concatenate([_dot(xb[:, m * bw:(m + 1) * bw], wi_ref[m]) for m in range(nblk)], axis=1)
    r = jax.nn.sigmoid(ga + ba_ref[...])
    i = jax.nn.sigmoid(gi + bi_ref[...])
    nl = -lam_ref[...]
    softplus = jnp.maximum(nl, 0.0) + jnp.log1p(jnp.exp(-jnp.abs(nl)))
    log_a = -RG_C * r * softplus
    a = jnp.exp(log_a)
    bx = jnp.sqrt(-jnp.tanh(log_a) * (a * a + 1.0)) * (i * xr)
    return a, bx


def _rglru_kernel(zr_ref, zg_ref, cw_ref, cb_ref, wa_ref, ba_ref, wi_ref, bi_ref, lam_ref,
                  ya_o, h_o, xs, h_sc, a_sc, b_sc, hs_sc, *, tt):
    t = pl.program_id(1)

    @pl.when(t == 0)
    def _():
        xs[0:8, :] = jnp.zeros((8, xs.shape[1]), F32)
        h_sc[...] = jnp.zeros(h_sc.shape, F32)

    x = zr_ref[0]
    xs[8:8 + tt, :] = x
    w = cw_ref[...]
    nk = w.shape[0]
    xr = cb_ref[...] + w[nk - 1:nk] * x
    for k in range(nk - 1):
        sh = nk - 1 - k
        xr = xr + w[k:k + 1] * xs[8 - sh:8 - sh + tt, :]
    xs[0:8, :] = xs[tt:tt + 8, :]

    a, bx = _rg_gates(xr, wa_ref, ba_ref, wi_ref, bi_ref, lam_ref)
    a_sc[...] = a
    b_sc[...] = bx

    def body(i, h):
        r0 = pl.multiple_of(i * 8, 8)
        a8 = a_sc[pl.ds(r0, 8), :]
        b8 = b_sc[pl.ds(r0, 8), :]
        rows = []
        for k in range(8):
            h = a8[k:k + 1] * h + b8[k:k + 1]
            rows.append(h)
        hs_sc[pl.ds(r0, 8), :] = jnp.concatenate(rows, axis=0)
        return h

    h = lax.fori_loop(0, tt // 8, body, h_sc[0:1, :])
    h_sc[0:1, :] = h
    h_o[0] = h
    ya_o[0] = (jax.nn.gelu(zg_ref[0]) * hs_sc[...]).astype(BF16)


def rglru_prompt(z, cw, cb, wa, ba, wi, bi, lam, tt):
    B, S, _ = z.shape
    D = D_MODEL
    nblk, bw = wa.shape[0], wa.shape[1]
    vec = pl.BlockSpec((1, D), lambda b, t: (0, 0))
    wspec = pl.BlockSpec((nblk, bw, bw), lambda b, t: (0, 0, 0))
    return pl.pallas_call(
        functools.partial(_rglru_kernel, tt=tt),
        grid=(B, S // tt),
        in_specs=[pl.BlockSpec((1, tt, D), lambda b, t: (b, t, COL_ZR // D)),
                  pl.BlockSpec((1, tt, D), lambda b, t: (b, t, COL_ZG // D)),
                  pl.BlockSpec(cw.shape, lambda b, t: (0, 0)), vec, wspec, vec, wspec, vec, vec],
        out_specs=[pl.BlockSpec((1, tt, D), lambda b, t: (b, t, 0)),
                   pl.BlockSpec((1, 1, D), lambda b, t: (b, 0, 0))],
        out_shape=[jax.ShapeDtypeStruct((B, S, D), BF16), jax.ShapeDtypeStruct((B, 1, D), F32)],
        scratch_shapes=[pltpu.VMEM((tt + 8, D), F32), pltpu.VMEM((8, D), F32),
                        pltpu.VMEM((tt, D), F32), pltpu.VMEM((tt, D), F32), pltpu.VMEM((tt, D), F32)],
        compiler_params=_cparams(("parallel", "arbitrary")),
        name="rglru_prompt",
    )(z, z, cw, cb, wa, ba, wi, bi, lam)


CF_PAD = 32
CF_ROWS = 64


def _layernorm_silu(u, g, b):
    mu = jnp.mean(u, axis=-1, keepdims=True)
    d = u - mu
    var = jnp.mean(d * d, axis=-1, keepdims=True)
    y = d * lax.rsqrt(var + EPS) * g + b
    return y * jax.nn.sigmoid(y)


def _conformer_kernel(za_ref, zb_ref, w_ref, b_ref, g_ref, be_ref, yb_o, st_o, us, u_sc, zs, *, tt):
    t = pl.program_id(1)
    D = us.shape[1]

    @pl.when(t == 0)
    def _():
        us[0:CF_PAD, :] = jnp.zeros((CF_PAD, D), F32)

    us[CF_PAD:CF_PAD + tt, :] = za_ref[0] * jax.nn.sigmoid(zb_ref[0])
    width = w_ref.shape[0]
    off = CF_PAD - (width - 1)
    for c in range(D // LANES):
        cs = slice(c * LANES, (c + 1) * LANES)
        for rs in range(tt // CF_ROWS):
            base = rs * CF_ROWS
            acc = jnp.broadcast_to(b_ref[:, cs], (CF_ROWS, LANES))
            for ph in range(8):
                rows = CF_ROWS if ph == 0 else CF_ROWS + 8
                z = None
                for k in range(width):
                    if (off + k) % 8 != ph:
                        continue
                    a0 = base + off + k - ph
                    term = w_ref[k:k + 1, cs] * us[a0:a0 + rows, cs]
                    z = term if z is None else z + term
                if z is None:
                    continue
                if ph == 0:
                    acc = acc + z
                else:
                    zs[ph] = z
                    acc = acc + zs[ph, ph:ph + CF_ROWS, :]
            u_sc[base:base + CF_ROWS, cs] = acc
    tail = us[tt:tt + CF_PAD, :]
    st_o[0] = tail
    us[0:CF_PAD, :] = tail
    yb_o[0] = _layernorm_silu(u_sc[...], g_ref[...], be_ref[...]).astype(BF16)


def conformer_prompt(z, w, b, g, be, tt):
    B, S, _ = z.shape
    D = D_MODEL
    vec = pl.BlockSpec((1, D), lambda bb, t: (0, 0))
    return pl.pallas_call(
        functools.partial(_conformer_kernel, tt=tt),
        grid=(B, S // tt),
        in_specs=[pl.BlockSpec((1, tt, D), lambda bb, t: (bb, t, COL_ZA // D)),
                  pl.BlockSpec((1, tt, D), lambda bb, t: (bb, t, COL_ZB // D)),
                  pl.BlockSpec(w.shape, lambda bb, t: (0, 0)), vec, vec, vec],
        out_specs=[pl.BlockSpec((1, tt, D), lambda bb, t: (bb, t, 0)),
                   pl.BlockSpec((1, CF_PAD, D), lambda bb, t: (bb, 0, 0))],
        out_shape=[jax.ShapeDtypeStruct((B, S, D), BF16), jax.ShapeDtypeStruct((B, CF_PAD, D), F32)],
        scratch_shapes=[pltpu.VMEM((tt + CF_PAD, D), F32), pltpu.VMEM((tt, D), F32),
                        pltpu.VMEM((8, CF_ROWS + 8, LANES), F32)],
        compiler_params=_cparams(("parallel", "arbitrary")),
        name="conformer_prompt",
    )(z, z, w, b, g, be)


def _sample_mix_kernel(zr_ref, zg_ref, za_ref, zb_ref, rgst_ref, h0_ref, cfst_ref,
                       cw_ref, cb_ref, wa_ref, ba_ref, wi_ref, bi_ref, lam_ref,
                       fw_ref, fb_ref, fg_ref, fbe_ref,
                       ya_o, yb_o, h_o, u_o):
    x = zr_ref[...]
    w = cw_ref[...]
    nk = w.shape[0]
    xr = cb_ref[...] + w[nk - 1:nk] * x
    for k in range(nk - 1):
        xr = xr + w[k:k + 1] * rgst_ref[k]
    a, bx = _rg_gates(xr, wa_ref, ba_ref, wi_ref, bi_ref, lam_ref)
    h = a * h0_ref[...] + bx
    h_o[...] = h
    ya_o[...] = (jax.nn.gelu(zg_ref[...]) * h).astype(BF16)

    u_new = za_ref[...] * jax.nn.sigmoid(zb_ref[...])
    u_o[...] = u_new
    width = fw_ref.shape[0]
    acc = fb_ref[...] + fw_ref[width - 1:width, :] * u_new
    for k in range(width - 1):
        acc = acc + fw_ref[k:k + 1, :] * cfst_ref[k]
    yb_o[...] = _layernorm_silu(acc, fg_ref[...], fbe_ref[...]).astype(BF16)


def sample_mix(z, rgst, h0, cfst, cw, cb, wa, ba, wi, bi, lam, fw, fb, fg, fbe):
    DB = z.shape[0]
    D = D_MODEL
    full = lambda a: pl.BlockSpec(a.shape, lambda i: (0,) * a.ndim)
    zs = lambda col: pl.BlockSpec((DB, D), lambda i: (0, col // D))
    o = pl.BlockSpec((DB, D), lambda i: (0, 0))
    return pl.pallas_call(
        _sample_mix_kernel,
        grid=(1,),
        in_specs=[zs(COL_ZR), zs(COL_ZG), zs(COL_ZA), zs(COL_ZB), full(rgst), full(h0), full(cfst),
                  full(cw), full(cb), full(wa), full(ba), full(wi), full(bi), full(lam),
                  full(fw), full(fb), full(fg), full(fbe)],
        out_specs=[o, o, o, o],
        out_shape=[jax.ShapeDtypeStruct((DB, D), BF16), jax.ShapeDtypeStruct((DB, D), BF16),
                   jax.ShapeDtypeStruct((DB, D), F32), jax.ShapeDtypeStruct((DB, D), F32)],
        compiler_params=_cparams(("arbitrary",)),
        name="sample_mix",
    )(z, z, z, z, rgst, h0, cfst, cw, cb, wa, ba, wi, bi, lam, fw, fb, fg, fbe)


CMP_PAGES = 16
TILE_PITCH = 72


def _compress_kernel(pages_ref, *refs):
    del pages_ref
    G, Dh = N_KV_HEADS, HEAD_DIM
    x_refs = refs[:CMP_PAGES]
    wk_ref, wv_ref, ck_o, cv_o, kbuf, vbuf = refs[CMP_PAGES:]
    for p in range(CMP_PAGES):
        for g in range(G):
            r0 = (p * G + g) * TILE_PITCH
            kbuf[r0:r0 + Dh, :] = x_refs[p][0, g]
            vbuf[r0:r0 + Dh, :] = x_refs[p][1, g]
    m = CMP_PAGES * G

    ak = jnp.zeros((m, PAGE), F32)
    av = jnp.zeros((m, PAGE), F32)
    for d in range(Dh):
        lk = kbuf[pl.ds(d, m, stride=TILE_PITCH), :].astype(BF16)
        lv = vbuf[pl.ds(d, m, stride=TILE_PITCH), :].astype(BF16)
        ak = ak + _dot(lk, wk_ref[d])
        av = av + _dot(lv, wv_ref[d])
    ck_o[...] = ak
    cv_o[...] = av


def compress_pages(xt, layer, pages, wk, wv):
    n = pages.shape[0]
    G, Dh = N_KV_HEADS, HEAD_DIM
    specs = [pl.BlockSpec((None, None, 2, G, Dh, PAGE),
                          (lambda s, pg, i=i: (layer, pg[s * CMP_PAGES + i], 0, 0, 0, 0)))
             for i in range(CMP_PAGES)]
    wspec = pl.BlockSpec(wk.shape, lambda s, pg: (0, 0, 0))
    ospec = pl.BlockSpec((CMP_PAGES * G, PAGE), lambda s, pg: (s, 0))
    grid_spec = pltpu.PrefetchScalarGridSpec(
        num_scalar_prefetch=1,
        grid=(n // CMP_PAGES,),
        in_specs=specs + [wspec, wspec],
        out_specs=[ospec, ospec],
        scratch_shapes=[pltpu.VMEM((CMP_PAGES * G * TILE_PITCH, PAGE), F32)] * 2,
    )
    oshape = jax.ShapeDtypeStruct((n * G, PAGE), F32)
    return pl.pallas_call(
        _compress_kernel,
        grid_spec=grid_spec,
        out_shape=[oshape, oshape],
        compiler_params=_cparams(("arbitrary",)),
        name="compress_pages",
    )(pages, *([xt] * CMP_PAGES), wk, wv)


def _prep_cmp_weight(w):
    wt = jnp.transpose(w, (1, 0, 2))
    eye = jnp.eye(2, dtype=w.dtype)
    t = jnp.einsum('dje,hk->dhjke', wt, eye)
    return t.reshape(w.shape[1], 2 * w.shape[0], 2 * w.shape[2]).astype(BF16)


def _block_major(o, lead, n_pages):
    G, Dh = N_KV_HEADS, HEAD_DIM
    o = o.reshape(lead, n_pages, G, 2, Dh)
    return jnp.transpose(o, (0, 2, 1, 3, 4)).reshape(lead, G, 2 * n_pages, Dh)


ONES_ROWS = 16


def _softmax_rows(s):
    m = jnp.max(s, axis=-1, keepdims=True)
    e = jnp.exp(s - m)
    return e / jnp.sum(e, axis=-1, keepdims=True)


def _rank_select(v, n_sel):
    n, q = v.shape
    sub = lax.broadcasted_iota(jnp.int32, (8, q), 0)
    groups = [v[8 * i:8 * i + 8] for i in range(n // 8)]
    ranks = [jnp.zeros((8, q), F32) for _ in groups]
    for j in range(n):
        rj = v[j:j + 1]
        for gi, vg in enumerate(groups):
            jj = j - 8 * gi
            if jj < 0:
                beats = rj >= vg
            elif jj >= 8:
                beats = rj > vg
            else:
                beats = (rj > vg) | ((rj == vg) & (sub > jj))
            ranks[gi] = ranks[gi] + jnp.where(beats, 1.0, 0.0)
    return jnp.where(jnp.concatenate(ranks, axis=0) < n_sel, 1.0, 0.0)


def _attn_prompt_kernel(q_ref, ck_ref, cvt_ref, ks_ref, vst_ref, kw_ref, vwt_ref, ng_ref, o_ref,
                        nsel_sc, s_sc, acc_sc, *, QB, KB):
    R, Dh = Q_PER_KV, HEAD_DIM
    NB = ck_ref.shape[2]
    pos0 = pl.program_id(2) * QB
    qblk = q_ref[0]
    qp = pos0 + lax.broadcasted_iota(jnp.int32, (1, QB), 1)
    blk = lax.broadcasted_iota(jnp.int32, (NB, 1), 0)

    cmask = ((blk + 1) * CMP_BLOCK - 1) <= qp
    cmf = jnp.where(cmask, 1.0, 0.0)
    ck, cvt = ck_ref[0, 0], cvt_ref[0, 0]
    imp = jnp.zeros((NB, QB), F32)
    o_cmp = []
    for r in range(R):
        s = jnp.where(cmask, _dot_nt(ck, qblk[:, r * Dh:(r + 1) * Dh]), NEG_INF)
        e = jnp.exp(s - jnp.max(s, axis=0, keepdims=True))
        p = e / jnp.sum(e, axis=0, keepdims=True) * cmf
        o_cmp.append(_dot(cvt, p.astype(BF16)))
        imp = imp + p
    imp = jnp.where(blk == qp // CMP_BLOCK, FORCE_LOCAL,
                    jnp.where(blk == 0, FORCE_INIT, jnp.where(blk * CMP_BLOCK <= qp, imp, -1.0)))
    nsel_sc[...] = jnp.where(_rank_select(imp, N_SEL) > 0.5, 0.0, NEG_INF)

    ones = jnp.ones((ONES_ROWS, KB), BF16)
    qs = [qblk[:, r * Dh:(r + 1) * Dh] for r in range(R)]
    bpt = KB // CMP_BLOCK
    n_t = (pos0 + QB) // KB

    def key_col(kt):
        return kt * KB + lax.broadcasted_iota(jnp.int32, (KB, 1), 0)

    def tile_max(s):
        return jnp.max(s.reshape(KB // 8, 8, QB), axis=0)

    def normalised(acc):
        return acc[:Dh] / acc[Dh:Dh + 1]

    unroll = 2 if s_sc.shape[1] % 2 == 0 else 1
    n_it = (n_t + unroll - 1) // unroll

    def score_tiles(it, mx):
        mx = list(mx)
        for u in range(unroll):
            kt = it * unroll + u
            k_t = ks_ref[0, 0, pl.ds(pl.multiple_of(kt * KB, KB), KB), :]
            bias = jnp.concatenate([jnp.broadcast_to(nsel_sc[pl.ds(kt * bpt + i, 1), :], (CMP_BLOCK, QB))
                                    for i in range(bpt)], axis=0)
            bias = bias + jnp.where(key_col(kt) <= qp, 0.0, NEG_INF)
            for r in range(R):
                s = _dot_nt(k_t, qs[r]) + bias
                s_sc[r, kt] = s
                mx[r] = jnp.maximum(mx[r], tile_max(s))
        return tuple(mx)

    mx = lax.fori_loop(0, n_it, score_tiles, tuple(jnp.full((8, QB), NEG_INF, F32) for _ in range(R)))
    m_sel = [jnp.max(x, axis=0, keepdims=True) for x in mx]

    acc_sc[...] = jnp.zeros(acc_sc.shape, F32)

    def value_tiles(it, carry):
        kts = [it * unroll + u for u in range(unroll)]
        v1s = [jnp.concatenate([vst_ref[0, 0, kt], ones], axis=0) for kt in kts]
        for r in range(R):
            part = None
            for kt, v1 in zip(kts, v1s):
                pv = _dot(v1, jnp.exp(s_sc[r, kt] - m_sel[r]).astype(BF16))
                part = pv if part is None else part + pv
            acc_sc[r] += part
        return carry

    lax.fori_loop(0, n_it, value_tiles, 0)

    kt_hi = n_t - 1
    w_tiles = []
    for i in range(WINDOW // KB + QB // KB):
        kt = kt_hi - i
        ktc = jnp.maximum(kt, 0)
        key = key_col(kt)
        d = qp - key
        bias = jnp.where(d >= 0, jnp.where(d <= WINDOW, jnp.where(key >= 0, 0.0, NEG_INF), NEG_INF), NEG_INF)
        k_t = kw_ref[0, 0, pl.ds(pl.multiple_of(ktc * KB, KB), KB), :]
        w_tiles.append((k_t, jnp.concatenate([vwt_ref[0, 0, ktc], ones], axis=0), bias))

    gt = jax.nn.sigmoid(ng_ref[0]).T
    ys = []
    for r in range(R):
        s_w = [_dot_nt(k_t, qs[r]) + bias for k_t, _, bias in w_tiles]
        mw = tile_max(s_w[0])
        for s in s_w[1:]:
            mw = jnp.maximum(mw, tile_max(s))
        mw = jnp.max(mw, axis=0, keepdims=True)
        acc_w = _dot(w_tiles[0][1], jnp.exp(s_w[0] - mw).astype(BF16))
        for (_, v1, _), s in zip(w_tiles[1:], s_w[1:]):
            acc_w = acc_w + _dot(v1, jnp.exp(s - mw).astype(BF16))
        ys.append(gt[3 * r:3 * r + 1] * o_cmp[r] + gt[3 * r + 1:3 * r + 2] * normalised(acc_sc[r])
                  + gt[3 * r + 2:3 * r + 3] * normalised(acc_w))
    o_ref[0] = jnp.concatenate(ys, axis=0).T.astype(BF16)


def attn_prompt(q, ck, cvt, ksg, vst, kwg, vwt, z, QB, KB):
    B, S, _ = q.shape
    G, R, Dh = N_KV_HEADS, Q_PER_KV, HEAD_DIM
    NB = ck.shape[2]
    kspec = pl.BlockSpec((1, 1, S, Dh), lambda b, g, c: (b, g, 0, 0))
    vspec = pl.BlockSpec((1, 1, S // KB, Dh, KB), lambda b, g, c: (b, g, 0, 0, 0))
    return pl.pallas_call(
        functools.partial(_attn_prompt_kernel, QB=QB, KB=KB),
        grid=(B, G, S // QB),
        in_specs=[pl.BlockSpec((1, QB, R * Dh), lambda b, g, c: (b, c, g)),
                  pl.BlockSpec((1, 1, NB, Dh), lambda b, g, c: (b, g, 0, 0)),
                  pl.BlockSpec((1, 1, Dh, NB), lambda b, g, c: (b, g, 0, 0)),
                  kspec, vspec, kspec, vspec,
                  pl.BlockSpec((1, QB, LANES), lambda b, g, c: (b, c, COL_NG // LANES + g))],
        out_specs=pl.BlockSpec((1, QB, R * Dh), lambda b, g, c: (b, c, g)),
        out_shape=jax.ShapeDtypeStruct((B, S, G * R * Dh), BF16),
        scratch_shapes=[pltpu.VMEM((NB, QB), F32), pltpu.VMEM((R, S // KB, KB, QB), F32),
                        pltpu.VMEM((R, Dh + ONES_ROWS, QB), F32)],
        compiler_params=_cparams(("parallel", "parallel", "arbitrary")),
        name="attn_prompt",
    )(q, ck, cvt, ksg, vst, kwg, vwt, z)


QPAD = 16


def _pad_rows(x, n):
    return jnp.concatenate([x, jnp.zeros((n - x.shape[0], x.shape[1]), x.dtype)], axis=0)


def _sample_cmp_kernel(q_ref, ck_ref, cv_ref, o_ref, idx_ref, *, NBC):
    G, R = N_KV_HEADS, Q_PER_KV
    for bi in range(q_ref.shape[0]):
        imps = []
        for g in range(G):
            q = _pad_rows(q_ref[bi, g], QPAD).astype(BF16)
            p = _softmax_rows(_dot_nt(q, ck_ref[bi, g]))
            o_ref[bi, g] = _dot(p.astype(BF16), cv_ref[bi, g])[:R]
            imps.append(p[0:1] + p[1:2] + p[2:3] + p[3:4])
        v = jnp.concatenate(imps + [jnp.full((8 - G, NBC), -1.0, F32)], axis=0)
        lane = lax.broadcasted_iota(jnp.int32, v.shape, 1)
        v = jnp.where(lane == 0, -1.0, v)
        out_lane = lax.broadcasted_iota(jnp.int32, (8, LANES), 1)
        idx = jnp.where(out_lane == N_SEL - 1, NBC, 0)
        for t in range(1, N_SEL - 1):
            m = jnp.max(v, axis=-1, keepdims=True)
            first = jnp.min(jnp.where(v == m, lane, NBC), axis=-1, keepdims=True)
            idx = jnp.where(out_lane == t, first, idx)
            v = jnp.where(lane == first, -2.0, v)
        idx_ref[bi] = idx


def sample_cmp(q, ck, cv):
    DB, G, R, Dh = q.shape
    NBC = ck.shape[2]
    bb = 8 if DB % 8 == 0 else 1
    return pl.pallas_call(
        functools.partial(_sample_cmp_kernel, NBC=NBC),
        grid=(DB // bb,),
        in_specs=[pl.BlockSpec((bb, G, R, Dh), lambda b: (b, 0, 0, 0)),
                  pl.BlockSpec((bb, G, NBC, Dh), lambda b: (b, 0, 0, 0)),
                  pl.BlockSpec((bb, G, NBC, Dh), lambda b: (b, 0, 0, 0))],
        out_specs=[pl.BlockSpec((bb, G, R, Dh), lambda b: (b, 0, 0, 0)),
                   pl.BlockSpec((bb, 8, LANES), lambda b: (b, 0, 0))],
        out_shape=[jax.ShapeDtypeStruct((DB, G, R, Dh), F32), jax.ShapeDtypeStruct((DB, 8, LANES), jnp.int32)],
        compiler_params=_cparams(("parallel",)),
        name="sample_cmp",
    )(q, ck, cv)


N_GATHER = N_SEL - 1


def _sample_attn_kernel(idx_ref, pt_ref, *refs):
    del pt_ref
    R, Dh = Q_PER_KV, HEAD_DIM
    blk_refs = refs[:N_GATHER]
    q_ref, ns_ref, win_ref, nw_ref, oc_ref, ng_ref, o_ref = refs[N_GATHER:]
    base = (pl.program_id(0) * pl.num_programs(1) + pl.program_id(1)) * N_SEL
    q = _pad_rows(q_ref[...], QPAD).astype(BF16)
    lane = lax.broadcasted_iota(jnp.int32, (1, LANES), 1)
    row0 = lax.broadcasted_iota(jnp.int32, (LANES, Dh), 0) == 0

    def new_tile(n_ref, kv):
        return jnp.where(row0, jnp.broadcast_to(n_ref[kv], (LANES, Dh)), 0.0).astype(BF16)

    def new_scores(n_ref):
        return jnp.where(lane == 0, _dot_nt(q, new_tile(n_ref, 0)), NEG_INF)

    tiles = []
    for k in range(N_GATHER):
        half = idx_ref[base + k] % 2
        s_k = _dot(q, blk_refs[k][0].astype(BF16))
        tiles.append(jnp.where(lane // CMP_BLOCK == half, s_k, NEG_INF))
    p = _softmax_rows(jnp.concatenate(tiles + [new_scores(ns_ref)], axis=1)).astype(BF16)
    o = _dot(p[:, N_GATHER * LANES:], new_tile(ns_ref, 1))
    for k in range(N_GATHER):
        o = o + _dot_nt(p[:, k * LANES:(k + 1) * LANES], blk_refs[k][1].astype(BF16))
    o_slc = o[:R]

    W = win_ref.shape[2]
    p = _softmax_rows(jnp.concatenate([_dot(q, win_ref[0].astype(BF16)), new_scores(nw_ref)], axis=1)).astype(BF16)
    o_win = (_dot_nt(p[:, :W], win_ref[1].astype(BF16)) + _dot(p[:, W:], new_tile(nw_ref, 1)))[:R]

    gates = jax.nn.sigmoid(ng_ref[...])
    o_cmp = oc_ref[...]
    rows = []
    for r in range(R):
        rows.append(gates[:, 3 * r:3 * r + 1] * o_cmp[r:r + 1] + gates[:, 3 * r + 1:3 * r + 2] * o_slc[r:r + 1]
                    + gates[:, 3 * r + 2:3 * r + 3] * o_win[r:r + 1])
    o_ref[...] = jnp.concatenate(rows, axis=0)


def sample_attn(idx, page_table, layer, slc_t, q, new_s, win_t, new_w, o_cmp, ng):
    DB, G, R, Dh = q.shape
    npg = page_table.shape[1]
    W = win_t.shape[5]
    pt = page_table.reshape(-1)

    def blk_map(k):
        def im(b, g, idx_r, pt_r):
            n = idx_r[(b * G + g) * N_SEL + k]
            return (layer, pt_r[b * npg + n // 2], 0, g, 0, 0)
        return im

    bg = lambda *tail: (lambda b, g, i, p: (b, g) + tail)
    grid_spec = pltpu.PrefetchScalarGridSpec(
        num_scalar_prefetch=2,
        grid=(DB, G),
        in_specs=[pl.BlockSpec((None, None, 2, None, Dh, PAGE), blk_map(k)) for k in range(N_GATHER)]
                 + [pl.BlockSpec((None, None, R, Dh), bg(0, 0)),
                    pl.BlockSpec((None, None, 2, 1, Dh), bg(0, 0, 0)),
                    pl.BlockSpec((None, None, 2, None, Dh, W), lambda b, g, i, p: (layer, b, 0, g, 0, 0)),
                    pl.BlockSpec((None, None, 2, 1, Dh), bg(0, 0, 0)),
                    pl.BlockSpec((None, None, R, Dh), bg(0, 0)),
                    pl.BlockSpec((None, None, 1, LANES), bg(0, 0))],
        out_specs=pl.BlockSpec((None, None, R, Dh), bg(0, 0)),
    )
    return pl.pallas_call(
        _sample_attn_kernel,
        grid_spec=grid_spec,
        out_shape=jax.ShapeDtypeStruct((DB, G, R, Dh), F32),
        compiler_params=_cparams(("arbitrary", "arbitrary")),
        name="sample_attn",
    )(idx, pt, *([slc_t] * N_GATHER), q, new_s, win_t, new_w, o_cmp, ng)


def _merge_kernel(x_ref, ya_ref, yb_ref, yc_ref, m0_ref, m1_ref, m2_ref, wa_ref, wb_ref, wc_ref, wo_ref, o_ref):
    m = (jax.nn.sigmoid(m0_ref[...]) * _dot(ya_ref[...], wa_ref[...])
         + jax.nn.sigmoid(m1_ref[...]) * _dot(yb_ref[...], wb_ref[...])
         + jax.nn.sigmoid(m2_ref[...]) * _dot(yc_ref[...], wc_ref[...]))
    o_ref[...] = x_ref[...] + _dot(m.astype(BF16), wo_ref[...])


def merge_out(x, ya, yb, yc, z, wa, wb, wc, wo, tm):
    T, D = x.shape
    row = pl.BlockSpec((tm, D), lambda i: (i, 0))
    zs = lambda k: pl.BlockSpec((tm, D), lambda i: (i, COL_MG // D + k))
    wsp = pl.BlockSpec((D, D), lambda i: (0, 0))
    return pl.pallas_call(
        _merge_kernel,
        grid=(T // tm,),
        in_specs=[row, row, row, row, zs(0), zs(1), zs(2), wsp, wsp, wsp, wsp],
        out_specs=row,
        out_shape=jax.ShapeDtypeStruct((T, D), F32),
        compiler_params=_cparams(("parallel",)),
        name="merge_out",
    )(x, ya, yb, yc, z, z, z, wa, wb, wc, wo)


def _ffn_kernel(x_ref, g_ref, wg_ref, wu_ref, wd_ref, o_ref, hn_sc, acc_sc):
    j = pl.program_id(1)

    @pl.when(j == 0)
    def _():
        x = x_ref[...]
        ms = jnp.mean(x * x, axis=-1, keepdims=True)
        hn_sc[...] = (x * lax.rsqrt(ms + EPS) * g_ref[...]).astype(BF16)
        acc_sc[...] = jnp.zeros(acc_sc.shape, F32)

    hn = hn_sc[...]
    fg = _dot(hn, wg_ref[...])
    fu = _dot(hn, wu_ref[...])
    act = (fg * jax.nn.sigmoid(fg)) * fu
    acc_sc[...] += _dot(act.astype(BF16), wd_ref[...])

    @pl.when(j == pl.num_programs(1) - 1)
    def _():
        o_ref[...] = x_ref[...] + acc_sc[...]


def ffn(x, gain, w_gu, w_down, tm, fc):
    T, D = x.shape
    F = w_down.shape[0]
    nf = F // fc
    return pl.pallas_call(
        _ffn_kernel,
        grid=(T // tm, nf),
        in_specs=[pl.BlockSpec((tm, D), lambda i, j: (i, 0)),
                  pl.BlockSpec((1, D), lambda i, j: (0, 0)),
                  pl.BlockSpec((D, fc), lambda i, j: (0, j)),
                  pl.BlockSpec((D, fc), lambda i, j: (0, nf + j)),
                  pl.BlockSpec((fc, D), lambda i, j: (j, 0))],
        out_specs=pl.BlockSpec((tm, D), lambda i, j: (i, 0)),
        out_shape=jax.ShapeDtypeStruct((T, D), F32),
        scratch_shapes=[pltpu.VMEM((tm, D), BF16), pltpu.VMEM((tm, D), F32)],
        compiler_params=_cparams(("parallel", "arbitrary")),
        name="ffn",
    )(x, gain.reshape(1, D), w_gu, w_gu, w_down)


def _rmsnorm_kernel(x_ref, g_ref, o_ref):
    x = x_ref[...]
    ms = jnp.mean(x * x, axis=-1, keepdims=True)
    o_ref[...] = x * lax.rsqrt(ms + EPS) * g_ref[...]


def rmsnorm_rows(x, gain, tm):
    T, D = x.shape
    return pl.pallas_call(
        _rmsnorm_kernel,
        grid=(T // tm,),
        in_specs=[pl.BlockSpec((tm, D), lambda i: (i, 0)), pl.BlockSpec((1, D), lambda i: (0, 0))],
        out_specs=pl.BlockSpec((tm, D), lambda i: (i, 0)),
        out_shape=jax.ShapeDtypeStruct((T, D), F32),
        compiler_params=_cparams(("parallel",)),
        name="final_rmsnorm",
    )(x, gain.reshape(1, D))


def _prep_w_in(w):
    D = w.shape[0]
    G, R = N_KV_HEADS, Q_PER_KV
    n_gate = 3 * G * R
    head, gates, tail = w[:, :COL_NG], w[:, COL_NG:COL_NG + n_gate], w[:, COL_NG + n_gate:]
    gates = gates.reshape(D, G, 3 * R)
    gates = jnp.pad(gates, ((0, 0), (0, 0), (0, LANES - 3 * R))).reshape(D, G * LANES)
    return jnp.concatenate([head, gates, tail], axis=1).astype(BF16)


RG_TILE = 256


def _prep_blockdiag(w):
    n, c, _ = w.shape
    per = RG_TILE // c
    w = w.reshape(n // per, per, c, c)
    eye = jnp.eye(per, dtype=w.dtype)
    t = jnp.einsum('mpcd,pq->mpcqd', w, eye)
    return t.reshape(n // per, RG_TILE, RG_TILE).astype(BF16)


def _tile(n, pref):
    t = min(n, pref)
    assert n % t == 0, (n, t)
    return t


def _token_minor(cache):
    nd = cache.ndim
    return jnp.transpose(cache, tuple(range(nd - 4)) + (nd - 3, nd - 2, nd - 1, nd - 4))


def kernel(x_prompt, x_sample, cache_cmp_kv, cache_slc_kv, cache_win_kv, state_rglru_conv, state_rglru_h,
           state_conf_conv, page_table, norm1_g, w_in, rg_conv_w, rg_conv_b, rg_wa, rg_ba, rg_wi, rg_bi,
           rg_lambda, cf_dw_w, cf_dw_b, cf_ln_g, cf_ln_b, nsa_w_ck, nsa_w_cv, w_br_rnn, w_br_conv, w_br_attn,
           w_out, norm2_g, w_ffn_gu, w_ffn_down, final_norm_g):
    B, S, D = x_prompt.shape
    DB = x_sample.shape[0]
    depth = w_in.shape[0]
    G, R, Dh = N_KV_HEADS, Q_PER_KV, HEAD_DIM
    n_pages = page_table.shape[1]
    past_len = n_pages * PAGE
    w_buf = cache_win_kv.shape[2]
    d_ff = w_ffn_down.shape[1]
    rg_k = rg_conv_w.shape[1]
    cf_k = cf_dw_w.shape[1]
    assert D == D_MODEL and x_sample.shape[1] == 1 and cache_cmp_kv.shape[2] == PAGE
    assert cache_cmp_kv.shape[3:] == (2, G, Dh) and w_buf == min(WINDOW, past_len)
    assert S % PAGE == 0 and cf_k - 1 <= CF_PAD

    T = B * S
    tm_p = _tile(T, 1024)
    tm_rope = _tile(S, 512)
    tt = _tile(S, 256)
    QB = _tile(S, 256)
    KB = _tile(S, 256)
    assert QB % KB == 0 and WINDOW % KB == 0 and KB % CMP_BLOCK == 0
    fc = d_ff // 2 if (d_ff // 2) % LANES == 0 else d_ff

    cos_p, sin_p = rope_tables(jnp.arange(S, dtype=jnp.int32))
    cos_s, sin_s = rope_tables(jnp.full((DB,), past_len, jnp.int32))
    prompt_pages = jnp.arange(T // PAGE, dtype=jnp.int32)
    sample_pages = page_table.reshape(-1).astype(jnp.int32)
    cmp_t, slc_t, win_t = _token_minor(cache_cmp_kv), _token_minor(cache_slc_kv), _token_minor(cache_win_kv)
    row2 = lambda v: v.reshape(1, -1)

    xp = x_prompt.reshape(T, D)
    xs = x_sample.reshape(DB, D)
    st_p, st_s = [], []
    for l in range(depth):
        w_in_l = _prep_w_in(w_in[l])
        wa_l, wi_l = _prep_blockdiag(rg_wa[l]), _prep_blockdiag(rg_wi[l])
        wk_c, wv_c = _prep_cmp_weight(nsa_w_ck[l]), _prep_cmp_weight(nsa_w_cv[l])
        w_a, w_b, w_c = w_br_rnn[l].astype(BF16), w_br_conv[l].astype(BF16), w_br_attn[l].astype(BF16)
        w_o, w_gu, w_dn = w_out[l].astype(BF16), w_ffn_gu[l].astype(BF16), w_ffn_down[l].astype(BF16)
        rg_args = (rg_conv_w[l], row2(rg_conv_b[l]), wa_l, row2(rg_ba[l]), wi_l, row2(rg_bi[l]),
                   row2(rg_lambda[l]))
        cf_args = (cf_dw_w[l], row2(cf_dw_b[l]), row2(cf_ln_g[l]), row2(cf_ln_b[l]))

        z = norm_matmul(xp, norm1_g[l], w_in_l, tm_p, 1024).reshape(B, S, N_PAD)
        ya, h_last = rglru_prompt(z, *rg_args, tt)
        yb, cf_tail = conformer_prompt(z, *cf_args, tt)
        q, kvc, kvs, kvw, kvc_t, ksg, vst, kwg, vwt = rope_split(z, cos_p, sin_p, tm_rope, KB)
        ck, cv = compress_pages(kvc_t.reshape(1, T // PAGE, 2, G, Dh, PAGE), 0, prompt_pages, wk_c, wv_c)
        ck = _block_major(ck, B, S // PAGE).astype(BF16)
        cvt = jnp.transpose(_block_major(cv, B, S // PAGE), (0, 1, 3, 2)).astype(BF16)
        yc = attn_prompt(q, ck, cvt, ksg, vst, kwg, vwt, z, QB, KB)
        xp = merge_out(xp, ya.reshape(T, D), yb.reshape(T, D), yc.reshape(T, D), z.reshape(T, N_PAD),
                       w_a, w_b, w_c, w_o, _tile(T, 512))
        xp = ffn(xp, norm2_g[l], w_gu, w_dn, _tile(T, 512), fc)
        kv6 = lambda a: a.reshape(B, S, 2, G, Dh)
        st_p.append((kv6(kvc), kv6(kvs), kv6(kvw)[:, S - w_buf:] if S >= w_buf else
                     jnp.pad(kv6(kvw), ((0, 0), (w_buf - S, 0), (0, 0), (0, 0), (0, 0))),
                     z[:, S - (rg_k - 1):, COL_ZR:COL_ZR + D], h_last.reshape(B, D),
                     cf_tail[:, CF_PAD - (cf_k - 1):]))

        zs = norm_matmul(xs, norm1_g[l], w_in_l, DB, 1024)
        rgst = jnp.transpose(state_rglru_conv[l], (1, 0, 2))
        cfst = jnp.transpose(state_conf_conv[l], (1, 0, 2))
        ya_s, yb_s, h_s, u_s = sample_mix(zs, rgst, state_rglru_h[l], cfst, *rg_args, *cf_args)
        q_s, kvc_s, kvs_s, kvw_s = rope_split(zs.reshape(1, DB, N_PAD), cos_s, sin_s, DB)
        q_s = q_s.reshape(DB, G, R, Dh).astype(F32)
        ck_s, cv_s = compress_pages(cmp_t, l, sample_pages, wk_c, wv_c)
        o_cmp_s, idx = sample_cmp(q_s, _block_major(ck_s, DB, n_pages).astype(BF16),
                                  _block_major(cv_s, DB, n_pages).astype(BF16))
        idx = idx[:, :G, :N_SEL].reshape(-1)
        new_rows = lambda a: jnp.transpose(a.reshape(DB, 2, G, 1, Dh), (0, 2, 1, 3, 4))
        ng_s = zs[:, COL_NG:COL_NG + G * LANES].reshape(DB, G, 1, LANES)
        yc_s = sample_attn(idx, page_table.astype(jnp.int32), l, slc_t, q_s, new_rows(kvs_s[0]), win_t,
                           new_rows(kvw_s[0]), o_cmp_s, ng_s)
        xs = merge_out(xs, ya_s, yb_s, yc_s.reshape(DB, D).astype(BF16), zs, w_a, w_b, w_c, w_o, DB)
        xs = ffn(xs, norm2_g[l], w_gu, w_dn, DB, fc)
        kv6s = lambda a: a.reshape(DB, 1, 2, G, Dh)
        st_s.append((kv6s(kvc_s), kv6s(kvs_s),
                     jnp.concatenate([cache_win_kv[l], kv6s(kvw_s)], axis=1)[:, 1:],
                     jnp.concatenate([state_rglru_conv[l], zs[:, None, COL_ZR:COL_ZR + D]], axis=1)[:, 1:],
                     h_s,
                     jnp.concatenate([state_conf_conv[l], u_s[:, None]], axis=1)[:, 1:]))

    y_prompt = rmsnorm_rows(xp, final_norm_g, tm_p).reshape(B, S, D)
    y_sample = rmsnorm_rows(xs, final_norm_g, DB).reshape(DB, 1, D)
    stk = lambda sts, j: jnp.stack([st[j] for st in sts], axis=0)
    return (y_prompt, y_sample,
            stk(st_p, 0), stk(st_p, 1), stk(st_p, 2), stk(st_p, 3), stk(st_p, 4), stk(st_p, 5),
            stk(st_s, 0), stk(st_s, 1), stk(st_s, 2), stk(st_s, 3), stk(st_s, 4), stk(st_s, 5))
```

```python
import functools

import jax
import jax.numpy as jnp
from jax import lax
from jax.experimental import pallas as pl
from jax.experimental.pallas import tpu as pltpu

F32 = jnp.float32
BF16 = jnp.bfloat16

CMP_BLOCK = 64
N_SEL = 16
WINDOW = 512
FORCE_LOCAL = 1e4
FORCE_INIT = 5e3
RG_C = 8.0
ROPE_THETA = 10000.0
EPS = 1e-6
NEG_INF = -1e30

HEAD_DIM = 64
N_KV_HEADS = 4
Q_PER_KV = 4
LANES = 128
VMEM_LIMIT = 56 * 1024 * 1024

D_MODEL = 1024
COL_ZR, COL_ZG, COL_ZA, COL_ZB, COL_ZQ = 0, 1024, 2048, 3072, 4096
COL_KVC, COL_KVS, COL_KVW = 5120, 5632, 6144
COL_NG = 6656
COL_MG = 7168
N_PAD = COL_MG + 3 * D_MODEL
KV_W = 2 * N_KV_HEADS * HEAD_DIM
PAGE = 2 * CMP_BLOCK


def _cparams(sem):
    return pltpu.CompilerParams(dimension_semantics=sem, vmem_limit_bytes=VMEM_LIMIT)


def _dot(a, b):
    return jnp.dot(a, b, preferred_element_type=F32)


def _dot_nt(a, b):
    return lax.dot_general(a, b, (((1,), (1,)), ((), ())), preferred_element_type=F32)


def _norm_mm_kernel(x_ref, g_ref, w_ref, o_ref, hn_sc):
    @pl.when(pl.program_id(1) == 0)
    def _():
        x = x_ref[...]
        ms = jnp.mean(x * x, axis=-1, keepdims=True)
        hn_sc[...] = (x * lax.rsqrt(ms + EPS) * g_ref[...]).astype(BF16)

    o_ref[...] = _dot(hn_sc[...], w_ref[...])


def norm_matmul(x, gain, w, tm, tn):
    T, D = x.shape
    N = w.shape[1]
    return pl.pallas_call(
        _norm_mm_kernel,
        grid=(T // tm, N // tn),
        in_specs=[pl.BlockSpec((tm, D), lambda i, j: (i, 0)),
                  pl.BlockSpec((1, D), lambda i, j: (0, 0)),
                  pl.BlockSpec((D, tn), lambda i, j: (0, j))],
        out_specs=pl.BlockSpec((tm, tn), lambda i, j: (i, j)),
        out_shape=jax.ShapeDtypeStruct((T, N), F32),
        scratch_shapes=[pltpu.VMEM((tm, D), BF16)],
        compiler_params=_cparams(("parallel", "arbitrary")),
        name="norm_matmul",
    )(x, gain.reshape(1, D), w)


def _rope_kernel(zq_ref, kc_ref, ks_ref, kw_ref, cos_ref, sin_ref, q_o, kvc_o, kvs_o, kvw_o, *group_outs, kb):
    cos = cos_ref[...]
    sin = sin_ref[...]

    def rot(x):
        w = x.shape[1]
        n = w // LANES
        c = jnp.concatenate([cos] * n, axis=1) if n > 1 else cos
        s = jnp.concatenate([sin] * n, axis=1) if n > 1 else sin
        lane = lax.broadcasted_iota(jnp.int32, x.shape, 1)
        first = (lane % HEAD_DIM) < (HEAD_DIM // 2)
        sw = jnp.where(first, pltpu.roll(x, w - HEAD_DIM // 2, 1), pltpu.roll(x, HEAD_DIM // 2, 1))
        return x * c + sw * s

    kd = N_KV_HEADS * HEAD_DIM
    q_o[0] = (rot(zq_ref[0]) * (HEAD_DIM ** -0.5)).astype(BF16)
    outs = []
    for src, dst in ((kc_ref, kvc_o), (ks_ref, kvs_o), (kw_ref, kvw_o)):
        x = src[0]
        k = rot(x[:, :kd])
        v = x[:, kd:]
        dst[0] = jnp.concatenate([k, v], axis=1)
        outs.append((k, v))
    if group_outs:
        cpt_o, ksg_o, vst_o, kwg_o, vwt_o = group_outs
        tm = kc_ref.shape[1]
        kc_t, vc_t = outs[0][0].T, outs[0][1].T
        vs_t, vw_t = outs[1][1].T, outs[2][1].T
        for g in range(N_KV_HEADS):
            sl = slice(g * HEAD_DIM, (g + 1) * HEAD_DIM)
            ksg_o[0, g] = outs[1][0][:, sl].astype(BF16)
            kwg_o[0, g] = outs[2][0][:, sl].astype(BF16)
            for j in range(tm // kb):
                vst_o[0, g, j] = vs_t[sl, j * kb:(j + 1) * kb].astype(BF16)
                vwt_o[0, g, j] = vw_t[sl, j * kb:(j + 1) * kb].astype(BF16)
            for j in range(tm // PAGE):
                cpt_o[0, j, 0, g] = kc_t[sl, j * PAGE:(j + 1) * PAGE]
                cpt_o[0, j, 1, g] = vc_t[sl, j * PAGE:(j + 1) * PAGE]


def rope_split(z, cos, sin, tm, kb=None):
    B, S, _ = z.shape
    G, Dh = N_KV_HEADS, HEAD_DIM
    nt = S // tm
    zspec = lambda w, col: pl.BlockSpec((1, tm, w), lambda b, i: (b, i, col // w))
    tab = pl.BlockSpec((tm, LANES), lambda b, i: (i, 0))
    extra_specs, extra_shapes = [], []
    if kb is not None:
        kspec = pl.BlockSpec((1, G, tm, Dh), lambda b, i: (b, 0, i, 0))
        vspec = pl.BlockSpec((1, G, tm // kb, Dh, kb), lambda b, i: (b, 0, i, 0, 0))
        k_shape = jax.ShapeDtypeStruct((B, G, S, Dh), BF16)
        v_shape = jax.ShapeDtypeStruct((B, G, S // kb, Dh, kb), BF16)
        extra_specs = [pl.BlockSpec((1, tm // PAGE, 2, G, Dh, PAGE), lambda b, i: (b, i, 0, 0, 0, 0)),
                       kspec, vspec, kspec, vspec]
        extra_shapes = [jax.ShapeDtypeStruct((B, S // PAGE, 2, G, Dh, PAGE), F32),
                        k_shape, v_shape, k_shape, v_shape]
    return pl.pallas_call(
        functools.partial(_rope_kernel, kb=kb),
        grid=(B, nt),
        in_specs=[zspec(D_MODEL, COL_ZQ), zspec(KV_W, COL_KVC), zspec(KV_W, COL_KVS), zspec(KV_W, COL_KVW),
                  tab, tab],
        out_specs=[pl.BlockSpec((1, tm, D_MODEL), lambda b, i: (b, i, 0))]
                  + [pl.BlockSpec((1, tm, KV_W), lambda b, i: (b, i, 0))] * 3 + extra_specs,
        out_shape=[jax.ShapeDtypeStruct((B, S, D_MODEL), BF16)]
                  + [jax.ShapeDtypeStruct((B, S, KV_W), F32)] * 3 + extra_shapes,
        compiler_params=_cparams(("parallel", "parallel")),
        name="rope_split",
    )(z, z, z, z, cos, sin)


def rope_tables(pos):
    half = HEAD_DIM // 2
    inv = ROPE_THETA ** (-jnp.arange(half, dtype=F32) / half)
    ang = pos.astype(F32)[:, None] * inv[None, :]
    cos, sin = jnp.cos(ang), jnp.sin(ang)
    cos_t = jnp.concatenate([cos, cos] * (LANES // HEAD_DIM), axis=1)
    sin_t = jnp.concatenate([-sin, sin] * (LANES // HEAD_DIM), axis=1)
    return cos_t, sin_t


def _rg_gates(xr, wa_ref, ba_ref, wi_ref, bi_ref, lam_ref):
    xb = xr.astype(BF16)
    nblk = wa_ref.shape[0]
    bw = wa_ref.shape[1]
    ga = jnp.concatenate([_dot(xb[:, m * bw:(m + 1) * bw], wa_ref[m]) for m in range(nblk)], axis=1)
    gi = jnp.concatenate([_dot(xb[:, m * bw:(m + 1) * bw], wi_ref[m]) for m in range(nblk)], axis=1)
    r = jax.nn.sigmoid(ga + ba_ref[...])
    i = jax.nn.sigmoid(gi + bi_ref[...])
    nl = -lam_ref[...]
    softplus = jnp.maximum(nl, 0.0) + jnp.log1p(jnp.exp(-jnp.abs(nl)))
    log_a = -RG_C * r * softplus
    a = jnp.exp(log_a)
    bx = jnp.sqrt(-jnp.tanh(log_a) * (a * a + 1.0)) * (i * xr)
    return a, bx


def _rglru_kernel(zr_ref, zg_ref, cw_ref, cb_ref, wa_ref, ba_ref, wi_ref, bi_ref, lam_ref,
                  ya_o, h_o, xs, h_sc, a_sc, b_sc, hs_sc, *, tt):
    t = pl.program_id(0)
    nb = zr_ref.shape[0]

    @pl.when(t == 0)
    def _():
        xs[:, 0:8, :] = jnp.zeros((nb, 8, xs.shape[2]), F32)
        h_sc[...] = jnp.zeros(h_sc.shape, F32)

    w = cw_ref[...]
    nk = w.shape[0]
    for b in range(nb):
        x = zr_ref[b]
        xs[b, 8:8 + tt, :] = x
        xr = cb_ref[...] + w[nk - 1:nk] * x
        for k in range(nk - 1):
            sh = nk - 1 - k
            xr = xr + w[k:k + 1] * xs[b, 8 - sh:8 - sh + tt, :]
        xs[b, 0:8, :] = xs[b, tt:tt + 8, :]
        a, bx = _rg_gates(xr, wa_ref, ba_ref, wi_ref, bi_ref, lam_ref)
        a_sc[b] = a
        b_sc[b] = bx

    def body(i, hs):
        r0 = pl.multiple_of(i * 8, 8)
        out = []
        for b in range(nb):
            h = hs[b]
            a8 = a_sc[b, pl.ds(r0, 8), :]
            b8 = b_sc[b, pl.ds(r0, 8), :]
            rows = []
            for k in range(8):
                h = a8[k:k + 1] * h + b8[k:k + 1]
                rows.append(h)
            hs_sc[b, pl.ds(r0, 8), :] = jnp.concatenate(rows, axis=0)
            out.append(h)
        return tuple(out)

    hs = lax.fori_loop(0, tt // 8, body, tuple(h_sc[b, 0:1, :] for b in range(nb)))
    for b in range(nb):
        h_sc[b, 0:1, :] = hs[b]
        h_o[b] = hs[b]
        ya_o[b] = (jax.nn.gelu(zg_ref[b]) * hs_sc[b]).astype(BF16)


def rglru_prompt(z, cw, cb, wa, ba, wi, bi, lam, tt):
    B, S, _ = z.shape
    D = D_MODEL
    nblk, bw = wa.shape[0], wa.shape[1]
    vec = pl.BlockSpec((1, D), lambda t: (0, 0))
    wspec = pl.BlockSpec((nblk, bw, bw), lambda t: (0, 0, 0))
    return pl.pallas_call(
        functools.partial(_rglru_kernel, tt=tt),
        grid=(S // tt,),
        in_specs=[pl.BlockSpec((B, tt, D), lambda t: (0, t, COL_ZR // D)),
                  pl.BlockSpec((B, tt, D), lambda t: (0, t, COL_ZG // D)),
                  pl.BlockSpec(cw.shape, lambda t: (0, 0)), vec, wspec, vec, wspec, vec, vec],
        out_specs=[pl.BlockSpec((B, tt, D), lambda t: (0, t, 0)),
                   pl.BlockSpec((B, 1, D), lambda t: (0, 0, 0))],
        out_shape=[jax.ShapeDtypeStruct((B, S, D), BF16), jax.ShapeDtypeStruct((B, 1, D), F32)],
        scratch_shapes=[pltpu.VMEM((B, tt + 8, D), F32), pltpu.VMEM((B, 8, D), F32),
                        pltpu.VMEM((B, tt, D), F32), pltpu.VMEM((B, tt, D), F32), pltpu.VMEM((B, tt, D), F32)],
        compiler_params=_cparams(("arbitrary",)),
        name="rglru_prompt",
    )(z, z, cw, cb, wa, ba, wi, bi, lam)


CF_PAD = 32
CF_ROWS = 64


def _layernorm_silu(u, g, b):
    mu = jnp.mean(u, axis=-1, keepdims=True)
    d = u - mu
    var = jnp.mean(d * d, axis=-1, keepdims=True)
    y = d * lax.rsqrt(var + EPS) * g + b
    return y * jax.nn.sigmoid(y)


def _conformer_kernel(za_ref, zb_ref, w_ref, b_ref, g_ref, be_ref, yb_o, st_o, us, u_sc, zs, *, tt):
    t = pl.program_id(1)
    D = us.shape[1]

    @pl.when(t == 0)
    def _():
        us[0:CF_PAD, :] = jnp.zeros((CF_PAD, D), F32)

    us[CF_PAD:CF_PAD + tt, :] = za_ref[0] * jax.nn.sigmoid(zb_ref[0])
    width = w_ref.shape[0]
    off = CF_PAD - (width - 1)
    for c in range(D // LANES):
        cs = slice(c * LANES, (c + 1) * LANES)
        for rs in range(tt // CF_ROWS):
            base = rs * CF_ROWS
            acc = jnp.broadcast_to(b_ref[:, cs], (CF_ROWS, LANES))
            for ph in range(8):
                rows = CF_ROWS if ph == 0 else CF_ROWS + 8
                z = None
                for k in range(width):
                    if (off + k) % 8 != ph:
                        continue
                    a0 = base + off + k - ph
                    term = w_ref[k:k + 1, cs] * us[a0:a0 + rows, cs]
                    z = term if z is None else z + term
                if z is None:
                    continue
                if ph == 0:
                    acc = acc + z
                else:
                    zs[ph] = z
                    acc = acc + zs[ph, ph:ph + CF_ROWS, :]
            u_sc[base:base + CF_ROWS, cs] = acc
    tail = us[tt:tt + CF_PAD, :]
    st_o[0] = tail
    us[0:CF_PAD, :] = tail
    yb_o[0] = _layernorm_silu(u_sc[...], g_ref[...], be_ref[...]).astype(BF16)


def conformer_prompt(z, w, b, g, be, tt):
    B, S, _ = z.shape
    D = D_MODEL
    vec = pl.BlockSpec((1, D), lambda bb, t: (0, 0))
    return pl.pallas_call(
        functools.partial(_conformer_kernel, tt=tt),
        grid=(B, S // tt),
        in_specs=[pl.BlockSpec((1, tt, D), lambda bb, t: (bb, t, COL_ZA // D)),
                  pl.BlockSpec((1, tt, D), lambda bb, t: (bb, t, COL_ZB // D)),
                  pl.BlockSpec(w.shape, lambda bb, t: (0, 0)), vec, vec, vec],
        out_specs=[pl.BlockSpec((1, tt, D), lambda bb, t: (bb, t, 0)),
                   pl.BlockSpec((1, CF_PAD, D), lambda bb, t: (bb, 0, 0))],
        out_shape=[jax.ShapeDtypeStruct((B, S, D), BF16), jax.ShapeDtypeStruct((B, CF_PAD, D), F32)],
        scratch_shapes=[pltpu.VMEM((tt + CF_PAD, D), F32), pltpu.VMEM((tt, D), F32),
                        pltpu.VMEM((8, CF_ROWS + 8, LANES), F32)],
        compiler_params=_cparams(("parallel", "arbitrary")),
        name="conformer_prompt",
    )(z, z, w, b, g, be)


def _sample_mix_kernel(zr_ref, zg_ref, za_ref, zb_ref, rgst_ref, h0_ref, cfst_ref,
                       cw_ref, cb_ref, wa_ref, ba_ref, wi_ref, bi_ref, lam_ref,
                       fw_ref, fb_ref, fg_ref, fbe_ref,
                       ya_o, yb_o, h_o, u_o):
    x = zr_ref[...]
    w = cw_ref[...]
    nk = w.shape[0]
    xr = cb_ref[...] + w[nk - 1:nk] * x
    for k in range(nk - 1):
        xr = xr + w[k:k + 1] * rgst_ref[k]
    a, bx = _rg_gates(xr, wa_ref, ba_ref, wi_ref, bi_ref, lam_ref)
    h = a * h0_ref[...] + bx
    h_o[...] = h
    ya_o[...] = (jax.nn.gelu(zg_ref[...]) * h).astype(BF16)

    u_new = za_ref[...] * jax.nn.sigmoid(zb_ref[...])
    u_o[...] = u_new
    width = fw_ref.shape[0]
    acc = fb_ref[...] + fw_ref[width - 1:width, :] * u_new
    for k in range(width - 1):
        acc = acc + fw_ref[k:k + 1, :] * cfst_ref[k]
    yb_o[...] = _layernorm_silu(acc, fg_ref[...], fbe_ref[...]).astype(BF16)


def sample_mix(z, rgst, h0, cfst, cw, cb, wa, ba, wi, bi, lam, fw, fb, fg, fbe):
    DB = z.shape[0]
    D = D_MODEL
    full = lambda a: pl.BlockSpec(a.shape, lambda i: (0,) * a.ndim)
    zs = lambda col: pl.BlockSpec((DB, D), lambda i: (0, col // D))
    o = pl.BlockSpec((DB, D), lambda i: (0, 0))
    return pl.pallas_call(
        _sample_mix_kernel,
        grid=(1,),
        in_specs=[zs(COL_ZR), zs(COL_ZG), zs(COL_ZA), zs(COL_ZB), full(rgst), full(h0), full(cfst),
                  full(cw), full(cb), full(wa), full(ba), full(wi), full(bi), full(lam),
                  full(fw), full(fb), full(fg), full(fbe)],
        out_specs=[o, o, o, o],
        out_shape=[jax.ShapeDtypeStruct((DB, D), BF16), jax.ShapeDtypeStruct((DB, D), BF16),
                   jax.ShapeDtypeStruct((DB, D), F32), jax.ShapeDtypeStruct((DB, D), F32)],
        compiler_params=_cparams(("arbitrary",)),
        name="sample_mix",
    )(z, z, z, z, rgst, h0, cfst, cw, cb, wa, ba, wi, bi, lam, fw, fb, fg, fbe)


CMP_PAGES = 16
TILE_PITCH = 72


def _compress_kernel(pages_ref, *refs):
    del pages_ref
    G, Dh = N_KV_HEADS, HEAD_DIM
    x_refs = refs[:CMP_PAGES]
    wk_ref, wv_ref, ck_o, cv_o, kbuf, vbuf = refs[CMP_PAGES:]
    for p in range(CMP_PAGES):
        for g in range(G):
            r0 = (p * G + g) * TILE_PITCH
            kbuf[r0:r0 + Dh, :] = x_refs[p][0, g]
            vbuf[r0:r0 + Dh, :] = x_refs[p][1, g]
    m = CMP_PAGES * G

    ak = jnp.zeros((m, PAGE), F32)
    av = jnp.zeros((m, PAGE), F32)
    for d in range(Dh):
        lk = kbuf[pl.ds(d, m, stride=TILE_PITCH), :].astype(BF16)
        lv = vbuf[pl.ds(d, m, stride=TILE_PITCH), :].astype(BF16)
        ak = ak + _dot(lk, wk_ref[d])
        av = av + _dot(lv, wv_ref[d])
    ck_o[...] = ak
    cv_o[...] = av


def compress_pages(xt, layer, pages, wk, wv):
    n = pages.shape[0]
    G, Dh = N_KV_HEADS, HEAD_DIM
    specs = [pl.BlockSpec((None, None, 2, G, Dh, PAGE),
                          (lambda s, pg, i=i: (layer, pg[s * CMP_PAGES + i], 0, 0, 0, 0)))
             for i in range(CMP_PAGES)]
    wspec = pl.BlockSpec(wk.shape, lambda s, pg: (0, 0, 0))
    ospec = pl.BlockSpec((CMP_PAGES * G, PAGE), lambda s, pg: (s, 0))
    grid_spec = pltpu.PrefetchScalarGridSpec(
        num_scalar_prefetch=1,
        grid=(n // CMP_PAGES,),
        in_specs=specs + [wspec, wspec],
        out_specs=[ospec, ospec],
        scratch_shapes=[pltpu.VMEM((CMP_PAGES * G * TILE_PITCH, PAGE), F32)] * 2,
    )
    oshape = jax.ShapeDtypeStruct((n * G, PAGE), F32)
    return pl.pallas_call(
        _compress_kernel,
        grid_spec=grid_spec,
        out_shape=[oshape, oshape],
        compiler_params=_cparams(("arbitrary",)),
        name="compress_pages",
    )(pages, *([xt] * CMP_PAGES), wk, wv)


def _prep_cmp_weight(w):
    wt = jnp.transpose(w, (1, 0, 2))
    eye = jnp.eye(2, dtype=w.dtype)
    t = jnp.einsum('dje,hk->dhjke', wt, eye)
    return t.reshape(w.shape[1], 2 * w.shape[0], 2 * w.shape[2]).astype(BF16)


def _block_major(o, lead, n_pages):
    G, Dh = N_KV_HEADS, HEAD_DIM
    o = o.reshape(lead, n_pages, G, 2, Dh)
    return jnp.transpose(o, (0, 2, 1, 3, 4)).reshape(lead, G, 2 * n_pages, Dh)


ONES_ROWS = 16


def _softmax_rows(s):
    m = jnp.max(s, axis=-1, keepdims=True)
    e = jnp.exp(s - m)
    return e / jnp.sum(e, axis=-1, keepdims=True)


def _rank_select(v, n_sel):
    n, q = v.shape
    sub = lax.broadcasted_iota(jnp.int32, (8, q), 0)
    groups = [v[8 * i:8 * i + 8] for i in range(n // 8)]
    ranks = [jnp.zeros((8, q), F32) for _ in groups]
    for j in range(n):
        rj = v[j:j + 1]
        for gi, vg in enumerate(groups):
            jj = j - 8 * gi
            if jj < 0:
                beats = rj >= vg
            elif jj >= 8:
                beats = rj > vg
            else:
                beats = (rj > vg) | ((rj == vg) & (sub > jj))
            ranks[gi] = ranks[gi] + jnp.where(beats, 1.0, 0.0)
    return jnp.where(jnp.concatenate(ranks, axis=0) < n_sel, 1.0, 0.0)


def _attn_prompt_kernel(q_ref, ck_ref, cvt_ref, ks_ref, vst_ref, kw_ref, vwt_ref, ng_ref, o_ref,
                        nsel_sc, s_sc, acc_sc, *, QB, KB):
    R, Dh = Q_PER_KV, HEAD_DIM
    NB = ck_ref.shape[2]
    pos0 = pl.program_id(2) * QB
    qblk = q_ref[0]
    qp = pos0 + lax.broadcasted_iota(jnp.int32, (1, QB), 1)
    blk = lax.broadcasted_iota(jnp.int32, (NB, 1), 0)

    cmask = ((blk + 1) * CMP_BLOCK - 1) <= qp
    cmf = jnp.where(cmask, 1.0, 0.0)
    ck, cvt = ck_ref[0, 0], cvt_ref[0, 0]
    imp = jnp.zeros((NB, QB), F32)
    o_cmp = []
    for r in range(R):
        s = jnp.where(cmask, _dot_nt(ck, qblk[:, r * Dh:(r + 1) * Dh]), NEG_INF)
        e = jnp.exp(s - jnp.max(s, axis=0, keepdims=True))
        p = e * (1.0 / jnp.sum(e, axis=0, keepdims=True)) * cmf
        o_cmp.append(_dot(cvt, p.astype(BF16)))
        imp = imp + p
    imp = jnp.where(blk == qp // CMP_BLOCK, FORCE_LOCAL,
                    jnp.where(blk == 0, FORCE_INIT, jnp.where(blk * CMP_BLOCK <= qp, imp, -1.0)))
    nsel_sc[...] = jnp.where(_rank_select(imp, N_SEL) > 0.5, 0.0, NEG_INF)

    ones = jnp.ones((ONES_ROWS, KB), BF16)
    qs = [qblk[:, r * Dh:(r + 1) * Dh] for r in range(R)]
    bpt = KB // CMP_BLOCK
    n_t = (pos0 + QB) // KB

    def key_col(kt):
        return kt * KB + lax.broadcasted_iota(jnp.int32, (KB, 1), 0)

    def tile_max(s):
        return jnp.max(s.reshape(KB // 8, 8, QB), axis=0)

    def normalised(acc):
        return acc[:Dh] * (1.0 / acc[Dh:Dh + 1])

    n_all = s_sc.shape[1]
    if n_all % 4 == 0:
        plan = [(4, n_t // 4), (2, (n_t % 4 + 1) // 2)]
    elif n_all % 2 == 0:
        plan = [(2, (n_t + 1) // 2)]
    else:
        plan = [(1, n_t)]

    def score_tiles(it, mx, base, unroll):
        mx = list(mx)
        for u in range(unroll):
            kt = base + it * unroll + u
            k_t = ks_ref[0, 0, pl.ds(pl.multiple_of(kt * KB, KB), KB), :]
            bias = jnp.concatenate([jnp.broadcast_to(nsel_sc[pl.ds(kt * bpt + i, 1), :], (CMP_BLOCK, QB))
                                    for i in range(bpt)], axis=0)
            bias = bias + jnp.where(key_col(kt) <= qp, 0.0, NEG_INF)
            for r in range(R):
                s = _dot_nt(k_t, qs[r]) + bias
                s_sc[r, kt] = s
                mx[r] = jnp.maximum(mx[r], tile_max(s))
        return tuple(mx)

    mx = tuple(jnp.full((8, QB), NEG_INF, F32) for _ in range(R))
    base = 0
    for unroll, trips in plan:
        mx = lax.fori_loop(0, trips, functools.partial(score_tiles, base=base, unroll=unroll), mx)
        base = base + unroll * trips
    m_sel = [jnp.max(x, axis=0, keepdims=True) for x in mx]

    acc_sc[...] = jnp.zeros(acc_sc.shape, F32)

    def value_tiles(it, carry, base, unroll):
        kts = [base + it * unroll + u for u in range(unroll)]
        v1s = [jnp.concatenate([vst_ref[0, 0, kt], ones], axis=0) for kt in kts]
        for r in range(R):
            part = None
            for kt, v1 in zip(kts, v1s):
                pv = _dot(v1, jnp.exp(s_sc[r, kt] - m_sel[r]).astype(BF16))
                part = pv if part is None else part + pv
            acc_sc[r] += part
        return carry

    base = 0
    for unroll, trips in plan:
        lax.fori_loop(0, trips, functools.partial(value_tiles, base=base, unroll=unroll), 0)
        base = base + unroll * trips

    kt_hi = n_t - 1
    w_tiles = []
    for i in range(WINDOW // KB + QB // KB):
        kt = kt_hi - i
        ktc = jnp.maximum(kt, 0)
        key = key_col(kt)
        d = qp - key
        bias = jnp.where(d >= 0, jnp.where(d <= WINDOW, jnp.where(key >= 0, 0.0, NEG_INF), NEG_INF), NEG_INF)
        k_t = kw_ref[0, 0, pl.ds(pl.multiple_of(ktc * KB, KB), KB), :]
        w_tiles.append((k_t, jnp.concatenate([vwt_ref[0, 0, ktc], ones], axis=0), bias))

    gt = jax.nn.sigmoid(ng_ref[0]).T
    ys = []
    for r in range(R):
        s_w = [_dot_nt(k_t, qs[r]) + bias for k_t, _, bias in w_tiles]
        mw = tile_max(s_w[0])
        for s in s_w[1:]:
            mw = jnp.maximum(mw, tile_max(s))
        mw = jnp.max(mw, axis=0, keepdims=True)
        acc_w = _dot(w_tiles[0][1], jnp.exp(s_w[0] - mw).astype(BF16))
        for (_, v1, _), s in zip(w_tiles[1:], s_w[1:]):
            acc_w = acc_w + _dot(v1, jnp.exp(s - mw).astype(BF16))
        ys.append(gt[3 * r:3 * r + 1] * o_cmp[r] + gt[3 * r + 1:3 * r + 2] * normalised(acc_sc[r])
                  + gt[3 * r + 2:3 * r + 3] * normalised(acc_w))
    o_ref[0] = jnp.concatenate(ys, axis=0).T.astype(BF16)


def attn_prompt(q, ck, cvt, ksg, vst, kwg, vwt, z, QB, KB):
    B, S, _ = q.shape
    G, R, Dh = N_KV_HEADS, Q_PER_KV, HEAD_DIM
    NB = ck.shape[2]
    kspec = pl.BlockSpec((1, 1, S, Dh), lambda b, g, c: (b, g, 0, 0))
    vspec = pl.BlockSpec((1, 1, S // KB, Dh, KB), lambda b, g, c: (b, g, 0, 0, 0))
    return pl.pallas_call(
        functools.partial(_attn_prompt_kernel, QB=QB, KB=KB),
        grid=(B, G, S // QB),
        in_specs=[pl.BlockSpec((1, QB, R * Dh), lambda b, g, c: (b, c, g)),
                  pl.BlockSpec((1, 1, NB, Dh), lambda b, g, c: (b, g, 0, 0)),
                  pl.BlockSpec((1, 1, Dh, NB), lambda b, g, c: (b, g, 0, 0)),
                  kspec, vspec, kspec, vspec,
                  pl.BlockSpec((1, QB, LANES), lambda b, g, c: (b, c, COL_NG // LANES + g))],
        out_specs=pl.BlockSpec((1, QB, R * Dh), lambda b, g, c: (b, c, g)),
        out_shape=jax.ShapeDtypeStruct((B, S, G * R * Dh), BF16),
        scratch_shapes=[pltpu.VMEM((NB, QB), F32), pltpu.VMEM((R, S // KB, KB, QB), F32),
                        pltpu.VMEM((R, Dh + ONES_ROWS, QB), F32)],
        compiler_params=_cparams(("parallel", "parallel", "arbitrary")),
        name="attn_prompt",
    )(q, ck, cvt, ksg, vst, kwg, vwt, z)


QPAD = 16


def _pad_rows(x, n):
    return jnp.concatenate([x, jnp.zeros((n - x.shape[0], x.shape[1]), x.dtype)], axis=0)


def _sample_cmp_kernel(q_ref, ck_ref, cv_ref, o_ref, idx_ref, *, NBC):
    G, R = N_KV_HEADS, Q_PER_KV
    for bi in range(q_ref.shape[0]):
        imps = []
        for g in range(G):
            q = _pad_rows(q_ref[bi, g], QPAD).astype(BF16)
            p = _softmax_rows(_dot_nt(q, ck_ref[bi, g]))
            o_ref[bi, g] = _dot(p.astype(BF16), cv_ref[bi, g])[:R]
            imps.append(p[0:1] + p[1:2] + p[2:3] + p[3:4])
        v = jnp.concatenate(imps + [jnp.full((8 - G, NBC), -1.0, F32)], axis=0)
        lane = lax.broadcasted_iota(jnp.int32, v.shape, 1)
        v = jnp.where(lane == 0, -1.0, v)
        out_lane = lax.broadcasted_iota(jnp.int32, (8, LANES), 1)
        idx = jnp.where(out_lane == N_SEL - 1, NBC, 0)
        for t in range(1, N_SEL - 1):
            m = jnp.max(v, axis=-1, keepdims=True)
            first = jnp.min(jnp.where(v == m, lane, NBC), axis=-1, keepdims=True)
            idx = jnp.where(out_lane == t, first, idx)
            v = jnp.where(lane == first, -2.0, v)
        idx_ref[bi] = idx


def sample_cmp(q, ck, cv):
    DB, G, R, Dh = q.shape
    NBC = ck.shape[2]
    bb = 8 if DB % 8 == 0 else 1
    return pl.pallas_call(
        functools.partial(_sample_cmp_kernel, NBC=NBC),
        grid=(DB // bb,),
        in_specs=[pl.BlockSpec((bb, G, R, Dh), lambda b: (b, 0, 0, 0)),
                  pl.BlockSpec((bb, G, NBC, Dh), lambda b: (b, 0, 0, 0)),
                  pl.BlockSpec((bb, G, NBC, Dh), lambda b: (b, 0, 0, 0))],
        out_specs=[pl.BlockSpec((bb, G, R, Dh), lambda b: (b, 0, 0, 0)),
                   pl.BlockSpec((bb, 8, LANES), lambda b: (b, 0, 0))],
        out_shape=[jax.ShapeDtypeStruct((DB, G, R, Dh), F32), jax.ShapeDtypeStruct((DB, 8, LANES), jnp.int32)],
        compiler_params=_cparams(("parallel",)),
        name="sample_cmp",
    )(q, ck, cv)


N_GATHER = N_SEL - 1


def _sample_attn_kernel(idx_ref, pt_ref, *refs):
    del pt_ref
    R, Dh = Q_PER_KV, HEAD_DIM
    blk_refs = refs[:N_GATHER]
    q_ref, ns_ref, win_ref, nw_ref, oc_ref, ng_ref, o_ref = refs[N_GATHER:]
    base = (pl.program_id(0) * pl.num_programs(1) + pl.program_id(1)) * N_SEL
    q = _pad_rows(q_ref[...], QPAD).astype(BF16)
    lane = lax.broadcasted_iota(jnp.int32, (1, LANES), 1)
    row0 = lax.broadcasted_iota(jnp.int32, (LANES, Dh), 0) == 0

    def new_tile(n_ref, kv):
        return jnp.where(row0, jnp.broadcast_to(n_ref[kv], (LANES, Dh)), 0.0).astype(BF16)

    def new_scores(n_ref):
        return jnp.where(lane == 0, _dot_nt(q, new_tile(n_ref, 0)), NEG_INF)

    tiles = []
    for k in range(N_GATHER):
        half = idx_ref[base + k] % 2
        s_k = _dot(q, blk_refs[k][0].astype(BF16))
        tiles.append(jnp.where(lane // CMP_BLOCK == half, s_k, NEG_INF))
    p = _softmax_rows(jnp.concatenate(tiles + [new_scores(ns_ref)], axis=1)).astype(BF16)
    o = _dot(p[:, N_GATHER * LANES:], new_tile(ns_ref, 1))
    for k in range(N_GATHER):
        o = o + _dot_nt(p[:, k * LANES:(k + 1) * LANES], blk_refs[k][1].astype(BF16))
    o_slc = o[:R]

    W = win_ref.shape[2]
    p = _softmax_rows(jnp.concatenate([_dot(q, win_ref[0].astype(BF16)), new_scores(nw_ref)], axis=1)).astype(BF16)
    o_win = (_dot_nt(p[:, :W], win_ref[1].astype(BF16)) + _dot(p[:, W:], new_tile(nw_ref, 1)))[:R]

    gates = jax.nn.sigmoid(ng_ref[...])
    o_cmp = oc_ref[...]
    rows = []
    for r in range(R):
        rows.append(gates[:, 3 * r:3 * r + 1] * o_cmp[r:r + 1] + gates[:, 3 * r + 1:3 * r + 2] * o_slc[r:r + 1]
                    + gates[:, 3 * r + 2:3 * r + 3] * o_win[r:r + 1])
    o_ref[...] = jnp.concatenate(rows, axis=0)


def sample_attn(idx, page_table, layer, slc_t, q, new_s, win_t, new_w, o_cmp, ng):
    DB, G, R, Dh = q.shape
    npg = page_table.shape[1]
    W = win_t.shape[5]
    pt = page_table.reshape(-1)

    def blk_map(k):
        def im(b, g, idx_r, pt_r):
            n = idx_r[(b * G + g) * N_SEL + k]
            return (layer, pt_r[b * npg + n // 2], 0, g, 0, 0)
        return im

    bg = lambda *tail: (lambda b, g, i, p: (b, g) + tail)
    grid_spec = pltpu.PrefetchScalarGridSpec(
        num_scalar_prefetch=2,
        grid=(DB, G),
        in_specs=[pl.BlockSpec((None, None, 2, None, Dh, PAGE), blk_map(k)) for k in range(N_GATHER)]
                 + [pl.BlockSpec((None, None, R, Dh), bg(0, 0)),
                    pl.BlockSpec((None, None, 2, 1, Dh), bg(0, 0, 0)),
                    pl.BlockSpec((None, None, 2, None, Dh, W), lambda b, g, i, p: (layer, b, 0, g, 0, 0)),
                    pl.BlockSpec((None, None, 2, 1, Dh), bg(0, 0, 0)),
                    pl.BlockSpec((None, None, R, Dh), bg(0, 0)),
                    pl.BlockSpec((None, None, 1, LANES), bg(0, 0))],
        out_specs=pl.BlockSpec((None, None, R, Dh), bg(0, 0)),
    )
    return pl.pallas_call(
        _sample_attn_kernel,
        grid_spec=grid_spec,
        out_shape=jax.ShapeDtypeStruct((DB, G, R, Dh), F32),
        compiler_params=_cparams(("arbitrary", "arbitrary")),
        name="sample_attn",
    )(idx, pt, *([slc_t] * N_GATHER), q, new_s, win_t, new_w, o_cmp, ng)


def _merge_kernel(x_ref, ya_ref, yb_ref, yc_ref, m0_ref, m1_ref, m2_ref, wa_ref, wb_ref, wc_ref, wo_ref, o_ref):
    m = (jax.nn.sigmoid(m0_ref[...]) * _dot(ya_ref[...], wa_ref[...])
         + jax.nn.sigmoid(m1_ref[...]) * _dot(yb_ref[...], wb_ref[...])
         + jax.nn.sigmoid(m2_ref[...]) * _dot(yc_ref[...], wc_ref[...]))
    o_ref[...] = x_ref[...] + _dot(m.astype(BF16), wo_ref[...])


def merge_out(x, ya, yb, yc, z, wa, wb, wc, wo, tm):
    T, D = x.shape
    row = pl.BlockSpec((tm, D), lambda i: (i, 0))
    zs = lambda k: pl.BlockSpec((tm, D), lambda i: (i, COL_MG // D + k))
    wsp = pl.BlockSpec((D, D), lambda i: (0, 0))
    return pl.pallas_call(
        _merge_kernel,
        grid=(T // tm,),
        in_specs=[row, row, row, row, zs(0), zs(1), zs(2), wsp, wsp, wsp, wsp],
        out_specs=row,
        out_shape=jax.ShapeDtypeStruct((T, D), F32),
        compiler_params=_cparams(("parallel",)),
        name="merge_out",
    )(x, ya, yb, yc, z, z, z, wa, wb, wc, wo)


def _ffn_kernel(x_ref, g_ref, wg_ref, wu_ref, wd_ref, o_ref, hn_sc, acc_sc):
    j = pl.program_id(1)

    @pl.when(j == 0)
    def _():
        x = x_ref[...]
        ms = jnp.mean(x * x, axis=-1, keepdims=True)
        hn_sc[...] = (x * lax.rsqrt(ms + EPS) * g_ref[...]).astype(BF16)
        acc_sc[...] = jnp.zeros(acc_sc.shape, F32)

    hn = hn_sc[...]
    fg = _dot(hn, wg_ref[...])
    fu = _dot(hn, wu_ref[...])
    act = (fg * jax.nn.sigmoid(fg)) * fu
    acc_sc[...] += _dot(act.astype(BF16), wd_ref[...])

    @pl.when(j == pl.num_programs(1) - 1)
    def _():
        o_ref[...] = x_ref[...] + acc_sc[...]


def ffn(x, gain, w_gu, w_down, tm, fc):
    T, D = x.shape
    F = w_down.shape[0]
    nf = F // fc
    return pl.pallas_call(
        _ffn_kernel,
        grid=(T // tm, nf),
        in_specs=[pl.BlockSpec((tm, D), lambda i, j: (i, 0)),
                  pl.BlockSpec((1, D), lambda i, j: (0, 0)),
                  pl.BlockSpec((D, fc), lambda i, j: (0, j)),
                  pl.BlockSpec((D, fc), lambda i, j: (0, nf + j)),
                  pl.BlockSpec((fc, D), lambda i, j: (j, 0))],
        out_specs=pl.BlockSpec((tm, D), lambda i, j: (i, 0)),
        out_shape=jax.ShapeDtypeStruct((T, D), F32),
        scratch_shapes=[pltpu.VMEM((tm, D), BF16), pltpu.VMEM((tm, D), F32)],
        compiler_params=_cparams(("parallel", "arbitrary")),
        name="ffn",
    )(x, gain.reshape(1, D), w_gu, w_gu, w_down)


def _rmsnorm_kernel(x_ref, g_ref, o_ref):
    x = x_ref[...]
    ms = jnp.mean(x * x, axis=-1, keepdims=True)
    o_ref[...] = x * lax.rsqrt(ms + EPS) * g_ref[...]


def rmsnorm_rows(x, gain, tm):
    T, D = x.shape
    return pl.pallas_call(
        _rmsnorm_kernel,
        grid=(T // tm,),
        in_specs=[pl.BlockSpec((tm, D), lambda i: (i, 0)), pl.BlockSpec((1, D), lambda i: (0, 0))],
        out_specs=pl.BlockSpec((tm, D), lambda i: (i, 0)),
        out_shape=jax.ShapeDtypeStruct((T, D), F32),
        compiler_params=_cparams(("parallel",)),
        name="final_rmsnorm",
    )(x, gain.reshape(1, D))


def _prep_w_in(w):
    D = w.shape[0]
    G, R = N_KV_HEADS, Q_PER_KV
    n_gate = 3 * G * R
    head, gates, tail = w[:, :COL_NG], w[:, COL_NG:COL_NG + n_gate], w[:, COL_NG + n_gate:]
    gates = gates.reshape(D, G, 3 * R)
    gates = jnp.pad(gates, ((0, 0), (0, 0), (0, LANES - 3 * R))).reshape(D, G * LANES)
    return jnp.concatenate([head, gates, tail], axis=1).astype(BF16)


RG_TILE = 256


def _prep_blockdiag(w):
    n, c, _ = w.shape
    per = RG_TILE // c
    w = w.reshape(n // per, per, c, c)
    eye = jnp.eye(per, dtype=w.dtype)
    t = jnp.einsum('mpcd,pq->mpcqd', w, eye)
    return t.reshape(n // per, RG_TILE, RG_TILE).astype(BF16)


def _tile(n, pref):
    t = min(n, pref)
    assert n % t == 0, (n, t)
    return t


def _token_minor(cache):
    nd = cache.ndim
    return jnp.transpose(cache, tuple(range(nd - 4)) + (nd - 3, nd - 2, nd - 1, nd - 4))


def kernel(x_prompt, x_sample, cache_cmp_kv, cache_slc_kv, cache_win_kv, state_rglru_conv, state_rglru_h,
           state_conf_conv, page_table, norm1_g, w_in, rg_conv_w, rg_conv_b, rg_wa, rg_ba, rg_wi, rg_bi,
           rg_lambda, cf_dw_w, cf_dw_b, cf_ln_g, cf_ln_b, nsa_w_ck, nsa_w_cv, w_br_rnn, w_br_conv, w_br_attn,
           w_out, norm2_g, w_ffn_gu, w_ffn_down, final_norm_g):
    B, S, D = x_prompt.shape
    DB = x_sample.shape[0]
    depth = w_in.shape[0]
    G, R, Dh = N_KV_HEADS, Q_PER_KV, HEAD_DIM
    n_pages = page_table.shape[1]
    past_len = n_pages * PAGE
    w_buf = cache_win_kv.shape[2]
    d_ff = w_ffn_down.shape[1]
    rg_k = rg_conv_w.shape[1]
    cf_k = cf_dw_w.shape[1]
    assert D == D_MODEL and x_sample.shape[1] == 1 and cache_cmp_kv.shape[2] == PAGE
    assert cache_cmp_kv.shape[3:] == (2, G, Dh) and w_buf == min(WINDOW, past_len)
    assert S % PAGE == 0 and cf_k - 1 <= CF_PAD

    T = B * S
    tm_p = _tile(T, 1024)
    tm_rope = _tile(S, 512)
    tt = _tile(S, 256)
    QB = _tile(S, 256)
    KB = _tile(S, 256)
    assert QB % KB == 0 and WINDOW % KB == 0 and KB % CMP_BLOCK == 0
    fc = d_ff // 2 if (d_ff // 2) % LANES == 0 else d_ff

    cos_p, sin_p = rope_tables(jnp.arange(S, dtype=jnp.int32))
    cos_s, sin_s = rope_tables(jnp.full((DB,), past_len, jnp.int32))
    prompt_pages = jnp.arange(T // PAGE, dtype=jnp.int32)
    sample_pages = page_table.reshape(-1).astype(jnp.int32)
    cmp_t, slc_t, win_t = _token_minor(cache_cmp_kv), _token_minor(cache_slc_kv), _token_minor(cache_win_kv)
    row2 = lambda v: v.reshape(1, -1)

    xp = x_prompt.reshape(T, D)
    xs = x_sample.reshape(DB, D)
    st_p, st_s = [], []
    for l in range(depth):
        w_in_l = _prep_w_in(w_in[l])
        wa_l, wi_l = _prep_blockdiag(rg_wa[l]), _prep_blockdiag(rg_wi[l])
        wk_c, wv_c = _prep_cmp_weight(nsa_w_ck[l]), _prep_cmp_weight(nsa_w_cv[l])
        w_a, w_b, w_c = w_br_rnn[l].astype(BF16), w_br_conv[l].astype(BF16), w_br_attn[l].astype(BF16)
        w_o, w_gu, w_dn = w_out[l].astype(BF16), w_ffn_gu[l].astype(BF16), w_ffn_down[l].astype(BF16)
        rg_args = (rg_conv_w[l], row2(rg_conv_b[l]), wa_l, row2(rg_ba[l]), wi_l, row2(rg_bi[l]),
                   row2(rg_lambda[l]))
        cf_args = (cf_dw_w[l], row2(cf_dw_b[l]), row2(cf_ln_g[l]), row2(cf_ln_b[l]))

        z = norm_matmul(xp, norm1_g[l], w_in_l, tm_p, 1024).reshape(B, S, N_PAD)
        ya, h_last = rglru_prompt(z, *rg_args, tt)
        yb, cf_tail = conformer_prompt(z, *cf_args, tt)
        q, kvc, kvs, kvw, kvc_t, ksg, vst, kwg, vwt = rope_split(z, cos_p, sin_p, tm_rope, KB)
        ck, cv = compress_pages(kvc_t.reshape(1, T // PAGE, 2, G, Dh, PAGE), 0, prompt_pages, wk_c, wv_c)
        ck = _block_major(ck, B, S // PAGE).astype(BF16)
        cvt = jnp.transpose(_block_major(cv, B, S // PAGE), (0, 1, 3, 2)).astype(BF16)
        yc = attn_prompt(q, ck, cvt, ksg, vst, kwg, vwt, z, QB, KB)
        xp = merge_out(xp, ya.reshape(T, D), yb.reshape(T, D), yc.reshape(T, D), z.reshape(T, N_PAD),
                       w_a, w_b, w_c, w_o, _tile(T, 512))
        xp = ffn(xp, norm2_g[l], w_gu, w_dn, _tile(T, 512), fc)
        kv6 = lambda a: a.reshape(B, S, 2, G, Dh)
        st_p.append((kv6(kvc), kv6(kvs), kv6(kvw)[:, S - w_buf:] if S >= w_buf else
                     jnp.pad(kv6(kvw), ((0, 0), (w_buf - S, 0), (0, 0), (0, 0), (0, 0))),
                     z[:, S - (rg_k - 1):, COL_ZR:COL_ZR + D], h_last.reshape(B, D),
                     cf_tail[:, CF_PAD - (cf_k - 1):]))

        zs = norm_matmul(xs, norm1_g[l], w_in_l, DB, 1024)
        rgst = jnp.transpose(state_rglru_conv[l], (1, 0, 2))
        cfst = jnp.transpose(state_conf_conv[l], (1, 0, 2))
        ya_s, yb_s, h_s, u_s = sample_mix(zs, rgst, state_rglru_h[l], cfst, *rg_args, *cf_args)
        q_s, kvc_s, kvs_s, kvw_s = rope_split(zs.reshape(1, DB, N_PAD), cos_s, sin_s, DB)
        q_s = q_s.reshape(DB, G, R, Dh).astype(F32)
        ck_s, cv_s = compress_pages(cmp_t, l, sample_pages, wk_c, wv_c)
        o_cmp_s, idx = sample_cmp(q_s, _block_major(ck_s, DB, n_pages).astype(BF16),
                                  _block_major(cv_s, DB, n_pages).astype(BF16))
        idx = idx[:, :G, :N_SEL].reshape(-1)
        new_rows = lambda a: jnp.transpose(a.reshape(DB, 2, G, 1, Dh), (0, 2, 1, 3, 4))
        ng_s = zs[:, COL_NG:COL_NG + G * LANES].reshape(DB, G, 1, LANES)
        yc_s = sample_attn(idx, page_table.astype(jnp.int32), l, slc_t, q_s, new_rows(kvs_s[0]), win_t,
                           new_rows(kvw_s[0]), o_cmp_s, ng_s)
        xs = merge_out(xs, ya_s, yb_s, yc_s.reshape(DB, D).astype(BF16), zs, w_a, w_b, w_c, w_o, DB)
        xs = ffn(xs, norm2_g[l], w_gu, w_dn, DB, fc)
        kv6s = lambda a: a.reshape(DB, 1, 2, G, Dh)
        st_s.append((kv6s(kvc_s), kv6s(kvs_s),
                     jnp.concatenate([cache_win_kv[l], kv6s(kvw_s)], axis=1)[:, 1:],
                     jnp.concatenate([state_rglru_conv[l], zs[:, None, COL_ZR:COL_ZR + D]], axis=1)[:, 1:],
                     h_s,
                     jnp.concatenate([state_conf_conv[l], u_s[:, None]], axis=1)[:, 1:]))

    y_prompt = rmsnorm_rows(xp, final_norm_g, tm_p).reshape(B, S, D)
    y_sample = rmsnorm_rows(xs, final_norm_g, DB).reshape(DB, 1, D)
    stk = lambda sts, j: jnp.stack([st[j] for st in sts], axis=0)
    return (y_prompt, y_sample,
            stk(st_p, 0), stk(st_p, 1), stk(st_p, 2), stk(st_p, 3), stk(st_p, 4), stk(st_p, 5),
            stk(st_s, 0), stk(st_s, 1), stk(st_s, 2), stk(st_s, 3), stk(st_s, 4), stk(st_s, 5))
```

```python
import functools

import jax
import jax.numpy as jnp
from jax import lax
from jax.experimental import pallas as pl
from jax.experimental.pallas import tpu as pltpu

F32 = jnp.float32
BF16 = jnp.bfloat16

CMP_BLOCK = 64
N_SEL = 16
WINDOW = 512
FORCE_LOCAL = 1e4
FORCE_INIT = 5e3
RG_C = 8.0
ROPE_THETA = 10000.0
EPS = 1e-6
NEG_INF = -1e30

HEAD_DIM = 64
N_KV_HEADS = 4
Q_PER_KV = 4
LANES = 128
VMEM_LIMIT = 56 * 1024 * 1024

D_MODEL = 1024
COL_ZR, COL_ZG, COL_ZA, COL_ZB, COL_ZQ = 0, 1024, 2048, 3072, 4096
COL_KVC, COL_KVS, COL_KVW = 5120, 5632, 6144
COL_NG = 6656
COL_MG = 7168
N_PAD = COL_MG + 3 * D_MODEL
KV_W = 2 * N_KV_HEADS * HEAD_DIM
PAGE = 2 * CMP_BLOCK


def _cparams(sem):
    return pltpu.CompilerParams(dimension_semantics=sem, vmem_limit_bytes=VMEM_LIMIT)


def _dot(a, b):
    return jnp.dot(a, b, preferred_element_type=F32)


def _dot_nt(a, b):
    return lax.dot_general(a, b, (((1,), (1,)), ((), ())), preferred_element_type=F32)


def _norm_mm_kernel(x_ref, g_ref, w_ref, o_ref, hn_sc):
    @pl.when(pl.program_id(1) == 0)
    def _():
        x = x_ref[...]
        ms = jnp.mean(x * x, axis=-1, keepdims=True)
        hn_sc[...] = (x * lax.rsqrt(ms + EPS) * g_ref[...]).astype(BF16)

    o_ref[...] = _dot(hn_sc[...], w_ref[...])


def norm_matmul(x, gain, w, tm, tn):
    T, D = x.shape
    N = w.shape[1]
    return pl.pallas_call(
        _norm_mm_kernel,
        grid=(T // tm, N // tn),
        in_specs=[pl.BlockSpec((tm, D), lambda i, j: (i, 0)),
                  pl.BlockSpec((1, D), lambda i, j: (0, 0)),
                  pl.BlockSpec((D, tn), lambda i, j: (0, j))],
        out_specs=pl.BlockSpec((tm, tn), lambda i, j: (i, j)),
        out_shape=jax.ShapeDtypeStruct((T, N), F32),
        scratch_shapes=[pltpu.VMEM((tm, D), BF16)],
        compiler_params=_cparams(("parallel", "arbitrary")),
        name="norm_matmul",
    )(x, gain.reshape(1, D), w)


def _rope_kernel(zq_ref, kc_ref, ks_ref, kw_ref, cos_ref, sin_ref, q_o, kvc_o, kvs_o, kvw_o, *group_outs, kb):
    cos = cos_ref[...]
    sin = sin_ref[...]

    def rot(x):
        w = x.shape[1]
        n = w // LANES
        c = jnp.concatenate([cos] * n, axis=1) if n > 1 else cos
        s = jnp.concatenate([sin] * n, axis=1) if n > 1 else sin
        lane = lax.broadcasted_iota(jnp.int32, x.shape, 1)
        first = (lane % HEAD_DIM) < (HEAD_DIM // 2)
        sw = jnp.where(first, pltpu.roll(x, w - HEAD_DIM // 2, 1), pltpu.roll(x, HEAD_DIM // 2, 1))
        return x * c + sw * s

    kd = N_KV_HEADS * HEAD_DIM
    q_o[0] = (rot(zq_ref[0]) * (HEAD_DIM ** -0.5)).astype(BF16)
    outs = []
    for src, dst in ((kc_ref, kvc_o), (ks_ref, kvs_o), (kw_ref, kvw_o)):
        x = src[0]
        k = rot(x[:, :kd])
        v = x[:, kd:]
        dst[0] = jnp.concatenate([k, v], axis=1)
        outs.append((k, v))
    if group_outs:
        cpt_o, ksg_o, vst_o, kwg_o, vwt_o = group_outs
        tm = kc_ref.shape[1]
        kc_t, vc_t = outs[0][0].T, outs[0][1].T
        vs_t, vw_t = outs[1][1].T, outs[2][1].T
        for g in range(N_KV_HEADS):
            sl = slice(g * HEAD_DIM, (g + 1) * HEAD_DIM)
            ksg_o[0, g] = outs[1][0][:, sl].astype(BF16)
            kwg_o[0, g] = outs[2][0][:, sl].astype(BF16)
            for j in range(tm // kb):
                vst_o[0, g, j] = vs_t[sl, j * kb:(j + 1) * kb].astype(BF16)
                vwt_o[0, g, j] = vw_t[sl, j * kb:(j + 1) * kb].astype(BF16)
            for j in range(tm // PAGE):
                cpt_o[0, j, 0, g] = kc_t[sl, j * PAGE:(j + 1) * PAGE]
                cpt_o[0, j, 1, g] = vc_t[sl, j * PAGE:(j + 1) * PAGE]


def rope_split(z, cos, sin, tm, kb=None):
    B, S, _ = z.shape
    G, Dh = N_KV_HEADS, HEAD_DIM
    nt = S // tm
    zspec = lambda w, col: pl.BlockSpec((1, tm, w), lambda b, i: (b, i, col // w))
    tab = pl.BlockSpec((tm, LANES), lambda b, i: (i, 0))
    extra_specs, extra_shapes = [], []
    if kb is not None:
        kspec = pl.BlockSpec((1, G, tm, Dh), lambda b, i: (b, 0, i, 0))
        vspec = pl.BlockSpec((1, G, tm // kb, Dh, kb), lambda b, i: (b, 0, i, 0, 0))
        k_shape = jax.ShapeDtypeStruct((B, G, S, Dh), BF16)
        v_shape = jax.ShapeDtypeStruct((B, G, S // kb, Dh, kb), BF16)
        extra_specs = [pl.BlockSpec((1, tm // PAGE, 2, G, Dh, PAGE), lambda b, i: (b, i, 0, 0, 0, 0)),
                       kspec, vspec, kspec, vspec]
        extra_shapes = [jax.ShapeDtypeStruct((B, S // PAGE, 2, G, Dh, PAGE), F32),
                        k_shape, v_shape, k_shape, v_shape]
    return pl.pallas_call(
        functools.partial(_rope_kernel, kb=kb),
        grid=(B, nt),
        in_specs=[zspec(D_MODEL, COL_ZQ), zspec(KV_W, COL_KVC), zspec(KV_W, COL_KVS), zspec(KV_W, COL_KVW),
                  tab, tab],
        out_specs=[pl.BlockSpec((1, tm, D_MODEL), lambda b, i: (b, i, 0))]
                  + [pl.BlockSpec((1, tm, KV_W), lambda b, i: (b, i, 0))] * 3 + extra_specs,
        out_shape=[jax.ShapeDtypeStruct((B, S, D_MODEL), BF16)]
                  + [jax.ShapeDtypeStruct((B, S, KV_W), F32)] * 3 + extra_shapes,
        compiler_params=_cparams(("parallel", "parallel")),
        name="rope_split",
    )(z, z, z, z, cos, sin)


def rope_tables(pos):
    half = HEAD_DIM // 2
    inv = ROPE_THETA ** (-jnp.arange(half, dtype=F32) / half)
    ang = pos.astype(F32)[:, None] * inv[None, :]
    cos, sin = jnp.cos(ang), jnp.sin(ang)
    cos_t = jnp.concatenate([cos, cos] * (LANES // HEAD_DIM), axis=1)
    sin_t = jnp.concatenate([-sin, sin] * (LANES // HEAD_DIM), axis=1)
    return cos_t, sin_t


def _rg_gates(xr, wa_ref, ba_ref, wi_ref, bi_ref, lam_ref):
    xb = xr.astype(BF16)
    nblk = wa_ref.shape[0]
    bw = wa_ref.shape[1]
    ga = jnp.concatenate([_dot(xb[:, m * bw:(m + 1) * bw], wa_ref[m]) for m in range(nblk)], axis=1)
    gi = jnp.concatenate([_dot(xb[:, m * bw:(m + 1) * bw], wi_ref[m]) for m in range(nblk)], axis=1)
    r = jax.nn.sigmoid(ga + ba_ref[...])
    i = jax.nn.sigmoid(gi + bi_ref[...])
    nl = -lam_ref[...]
    softplus = jnp.maximum(nl, 0.0) + jnp.log1p(jnp.exp(-jnp.abs(nl)))
    log_a = -RG_C * r * softplus
    a = jnp.exp(log_a)
    bx = jnp.sqrt(-jnp.tanh(log_a) * (a * a + 1.0)) * (i * xr)
    return a, bx


def _rglru_kernel(zr_ref, zg_ref, cw_ref, cb_ref, wa_ref, ba_ref, wi_ref, bi_ref, lam_ref,
                  ya_o, h_o, xs, h_sc, a_sc, b_sc, hs_sc, *, tt):
    t = pl.program_id(0)
    nb = zr_ref.shape[0]

    @pl.when(t == 0)
    def _():
        xs[:, 0:8, :] = jnp.zeros((nb, 8, xs.shape[2]), F32)
        h_sc[...] = jnp.zeros(h_sc.shape, F32)

    w = cw_ref[...]
    nk = w.shape[0]
    for b in range(nb):
        x = zr_ref[b]
        xs[b, 8:8 + tt, :] = x
        xr = cb_ref[...] + w[nk - 1:nk] * x
        for k in range(nk - 1):
            sh = nk - 1 - k
            xr = xr + w[k:k + 1] * xs[b, 8 - sh:8 - sh + tt, :]
        xs[b, 0:8, :] = xs[b, tt:tt + 8, :]
        a, bx = _rg_gates(xr, wa_ref, ba_ref, wi_ref, bi_ref, lam_ref)
        a_sc[b] = a
        b_sc[b] = bx

    def body(i, hs):
        r0 = pl.multiple_of(i * 8, 8)
        out = []
        for b in range(nb):
            h = hs[b]
            a8 = a_sc[b, pl.ds(r0, 8), :]
            b8 = b_sc[b, pl.ds(r0, 8), :]
            rows = []
            for k in range(8):
                h = a8[k:k + 1] * h + b8[k:k + 1]
                rows.append(h)
            hs_sc[b, pl.ds(r0, 8), :] = jnp.concatenate(rows, axis=0)
            out.append(h)
        return tuple(out)

    hs = lax.fori_loop(0, tt // 8, body, tuple(h_sc[b, 0:1, :] for b in range(nb)))
    for b in range(nb):
        h_sc[b, 0:1, :] = hs[b]
        h_o[b] = hs[b]
        ya_o[b] = (jax.nn.gelu(zg_ref[b]) * hs_sc[b]).astype(BF16)


def rglru_prompt(z, cw, cb, wa, ba, wi, bi, lam, tt):
    B, S, _ = z.shape
    D = D_MODEL
    nblk, bw = wa.shape[0], wa.shape[1]
    vec = pl.BlockSpec((1, D), lambda t: (0, 0))
    wspec = pl.BlockSpec((nblk, bw, bw), lambda t: (0, 0, 0))
    return pl.pallas_call(
        functools.partial(_rglru_kernel, tt=tt),
        grid=(S // tt,),
        in_specs=[pl.BlockSpec((B, tt, D), lambda t: (0, t, COL_ZR // D)),
                  pl.BlockSpec((B, tt, D), lambda t: (0, t, COL_ZG // D)),
                  pl.BlockSpec(cw.shape, lambda t: (0, 0)), vec, wspec, vec, wspec, vec, vec],
        out_specs=[pl.BlockSpec((B, tt, D), lambda t: (0, t, 0)),
                   pl.BlockSpec((B, 1, D), lambda t: (0, 0, 0))],
        out_shape=[jax.ShapeDtypeStruct((B, S, D), BF16), jax.ShapeDtypeStruct((B, 1, D), F32)],
        scratch_shapes=[pltpu.VMEM((B, tt + 8, D), F32), pltpu.VMEM((B, 8, D), F32),
                        pltpu.VMEM((B, tt, D), F32), pltpu.VMEM((B, tt, D), F32), pltpu.VMEM((B, tt, D), F32)],
        compiler_params=_cparams(("arbitrary",)),
        name="rglru_prompt",
    )(z, z, cw, cb, wa, ba, wi, bi, lam)


CF_PAD = 32
CF_ROWS = 64


def _layernorm_silu(u, g, b):
    mu = jnp.mean(u, axis=-1, keepdims=True)
    d = u - mu
    var = jnp.mean(d * d, axis=-1, keepdims=True)
    y = d * lax.rsqrt(var + EPS) * g + b
    return y * jax.nn.sigmoid(y)


def _conformer_kernel(za_ref, zb_ref, w_ref, b_ref, g_ref, be_ref, yb_o, st_o, us, u_sc, zs, *, tt):
    t = pl.program_id(1)
    D = us.shape[1]

    @pl.when(t == 0)
    def _():
        us[0:CF_PAD, :] = jnp.zeros((CF_PAD, D), F32)

    us[CF_PAD:CF_PAD + tt, :] = za_ref[0] * jax.nn.sigmoid(zb_ref[0])
    width = w_ref.shape[0]
    off = CF_PAD - (width - 1)
    for c in range(D // LANES):
        cs = slice(c * LANES, (c + 1) * LANES)
        for rs in range(tt // CF_ROWS):
            base = rs * CF_ROWS
            acc = jnp.broadcast_to(b_ref[:, cs], (CF_ROWS, LANES))
            for ph in range(8):
                rows = CF_ROWS if ph == 0 else CF_ROWS + 8
                z = None
                for k in range(width):
                    if (off + k) % 8 != ph:
                        continue
                    a0 = base + off + k - ph
                    term = w_ref[k:k + 1, cs] * us[a0:a0 + rows, cs]
                    z = term if z is None else z + term
                if z is None:
                    continue
                if ph == 0:
                    acc = acc + z
                else:
                    zs[ph] = z
                    acc = acc + zs[ph, ph:ph + CF_ROWS, :]
            u_sc[base:base + CF_ROWS, cs] = acc
    tail = us[tt:tt + CF_PAD, :]
    st_o[0] = tail
    us[0:CF_PAD, :] = tail
    yb_o[0] = _layernorm_silu(u_sc[...], g_ref[...], be_ref[...]).astype(BF16)


def conformer_prompt(z, w, b, g, be, tt):
    B, S, _ = z.shape
    D = D_MODEL
    vec = pl.BlockSpec((1, D), lambda bb, t: (0, 0))
    return pl.pallas_call(
        functools.partial(_conformer_kernel, tt=tt),
        grid=(B, S // tt),
        in_specs=[pl.BlockSpec((1, tt, D), lambda bb, t: (bb, t, COL_ZA // D)),
                  pl.BlockSpec((1, tt, D), lambda bb, t: (bb, t, COL_ZB // D)),
                  pl.BlockSpec(w.shape, lambda bb, t: (0, 0)), vec, vec, vec],
        out_specs=[pl.BlockSpec((1, tt, D), lambda bb, t: (bb, t, 0)),
                   pl.BlockSpec((1, CF_PAD, D), lambda bb, t: (bb, 0, 0))],
        out_shape=[jax.ShapeDtypeStruct((B, S, D), BF16), jax.ShapeDtypeStruct((B, CF_PAD, D), F32)],
        scratch_shapes=[pltpu.VMEM((tt + CF_PAD, D), F32), pltpu.VMEM((tt, D), F32),
                        pltpu.VMEM((8, CF_ROWS + 8, LANES), F32)],
        compiler_params=_cparams(("parallel", "arbitrary")),
        name="conformer_prompt",
    )(z, z, w, b, g, be)


def _sample_mix_kernel(zr_ref, zg_ref, za_ref, zb_ref, rgst_ref, h0_ref, cfst_ref,
                       cw_ref, cb_ref, wa_ref, ba_ref, wi_ref, bi_ref, lam_ref,
                       fw_ref, fb_ref, fg_ref, fbe_ref,
                       ya_o, yb_o, h_o, u_o):
    x = zr_ref[...]
    w = cw_ref[...]
    nk = w.shape[0]
    xr = cb_ref[...] + w[nk - 1:nk] * x
    for k in range(nk - 1):
        xr = xr + w[k:k + 1] * rgst_ref[k]
    a, bx = _rg_gates(xr, wa_ref, ba_ref, wi_ref, bi_ref, lam_ref)
    h = a * h0_ref[...] + bx
    h_o[...] = h
    ya_o[...] = (jax.nn.gelu(zg_ref[...]) * h).astype(BF16)

    u_new = za_ref[...] * jax.nn.sigmoid(zb_ref[...])
    u_o[...] = u_new
    width = fw_ref.shape[0]
    acc = fb_ref[...] + fw_ref[width - 1:width, :] * u_new
    for k in range(width - 1):
        acc = acc + fw_ref[k:k + 1, :] * cfst_ref[k]
    yb_o[...] = _layernorm_silu(acc, fg_ref[...], fbe_ref[...]).astype(BF16)


def sample_mix(z, rgst, h0, cfst, cw, cb, wa, ba, wi, bi, lam, fw, fb, fg, fbe):
    DB = z.shape[0]
    D = D_MODEL
    full = lambda a: pl.BlockSpec(a.shape, lambda i: (0,) * a.ndim)
    zs = lambda col: pl.BlockSpec((DB, D), lambda i: (0, col // D))
    o = pl.BlockSpec((DB, D), lambda i: (0, 0))
    return pl.pallas_call(
        _sample_mix_kernel,
        grid=(1,),
        in_specs=[zs(COL_ZR), zs(COL_ZG), zs(COL_ZA), zs(COL_ZB), full(rgst), full(h0), full(cfst),
                  full(cw), full(cb), full(wa), full(ba), full(wi), full(bi), full(lam),
                  full(fw), full(fb), full(fg), full(fbe)],
        out_specs=[o, o, o, o],
        out_shape=[jax.ShapeDtypeStruct((DB, D), BF16), jax.ShapeDtypeStruct((DB, D), BF16),
                   jax.ShapeDtypeStruct((DB, D), F32), jax.ShapeDtypeStruct((DB, D), F32)],
        compiler_params=_cparams(("arbitrary",)),
        name="sample_mix",
    )(z, z, z, z, rgst, h0, cfst, cw, cb, wa, ba, wi, bi, lam, fw, fb, fg, fbe)


CMP_PAGES = 16
TILE_PITCH = 72


def _compress_kernel(pages_ref, *refs, n_pg):
    del pages_ref
    G, Dh = N_KV_HEADS, HEAD_DIM
    x_refs = refs[:n_pg]
    wk_ref, wv_ref, ck_o, cv_o, kbuf, vbuf = refs[n_pg:]
    for p in range(n_pg):
        for g in range(G):
            r0 = (p * G + g) * TILE_PITCH
            kbuf[r0:r0 + Dh, :] = x_refs[p][0, g]
            vbuf[r0:r0 + Dh, :] = x_refs[p][1, g]
    m = n_pg * G

    ak = jnp.zeros((m, PAGE), F32)
    av = jnp.zeros((m, PAGE), F32)
    for d in range(Dh):
        lk = kbuf[pl.ds(d, m, stride=TILE_PITCH), :].astype(BF16)
        lv = vbuf[pl.ds(d, m, stride=TILE_PITCH), :].astype(BF16)
        ak = ak + _dot(lk, wk_ref[d])
        av = av + _dot(lv, wv_ref[d])
    ck_o[...] = ak
    cv_o[...] = av


def compress_pages(xt, layer, pages, wk, wv):
    n = pages.shape[0]
    G, Dh = N_KV_HEADS, HEAD_DIM
    n_pg = 2 * CMP_PAGES if n % (2 * CMP_PAGES) == 0 else CMP_PAGES
    specs = [pl.BlockSpec((None, None, 2, G, Dh, PAGE),
                          (lambda s, pg, i=i: (layer, pg[s * n_pg + i], 0, 0, 0, 0)))
             for i in range(n_pg)]
    wspec = pl.BlockSpec(wk.shape, lambda s, pg: (0, 0, 0))
    ospec = pl.BlockSpec((n_pg * G, PAGE), lambda s, pg: (s, 0))
    grid_spec = pltpu.PrefetchScalarGridSpec(
        num_scalar_prefetch=1,
        grid=(n // n_pg,),
        in_specs=specs + [wspec, wspec],
        out_specs=[ospec, ospec],
        scratch_shapes=[pltpu.VMEM((n_pg * G * TILE_PITCH, PAGE), F32)] * 2,
    )
    oshape = jax.ShapeDtypeStruct((n * G, PAGE), F32)
    return pl.pallas_call(
        functools.partial(_compress_kernel, n_pg=n_pg),
        grid_spec=grid_spec,
        out_shape=[oshape, oshape],
        compiler_params=_cparams(("arbitrary",)),
        name="compress_pages",
    )(pages, *([xt] * n_pg), wk, wv)


def _prep_cmp_weight(w):
    wt = jnp.transpose(w, (1, 0, 2))
    eye = jnp.eye(2, dtype=w.dtype)
    t = jnp.einsum('dje,hk->dhjke', wt, eye)
    return t.reshape(w.shape[1], 2 * w.shape[0], 2 * w.shape[2]).astype(BF16)


def _block_major(o, lead, n_pages):
    G, Dh = N_KV_HEADS, HEAD_DIM
    o = o.reshape(lead, n_pages, G, 2, Dh)
    return jnp.transpose(o, (0, 2, 1, 3, 4)).reshape(lead, G, 2 * n_pages, Dh)


ONES_ROWS = 16


def _softmax_rows(s):
    m = jnp.max(s, axis=-1, keepdims=True)
    e = jnp.exp(s - m)
    return e / jnp.sum(e, axis=-1, keepdims=True)


def _rank_select(v, n_sel):
    n, q = v.shape
    sub = lax.broadcasted_iota(jnp.int32, (8, q), 0)
    groups = [v[8 * i:8 * i + 8] for i in range(n // 8)]
    ranks = [jnp.zeros((8, q), F32) for _ in groups]
    for j in range(n):
        rj = v[j:j + 1]
        for gi, vg in enumerate(groups):
            jj = j - 8 * gi
            if jj < 0:
                beats = rj >= vg
            elif jj >= 8:
                beats = rj > vg
            else:
                beats = (rj > vg) | ((rj == vg) & (sub > jj))
            ranks[gi] = ranks[gi] + jnp.where(beats, 1.0, 0.0)
    return jnp.where(jnp.concatenate(ranks, axis=0) < n_sel, 1.0, 0.0)


def _attn_prompt_kernel(q_ref, ck_ref, cvt_ref, ks_ref, vst_ref, kw_ref, vwt_ref, ng_ref, o_ref,
                        nsel_sc, s_sc, acc_sc, *, QB, KB):
    R, Dh = Q_PER_KV, HEAD_DIM
    NB = ck_ref.shape[2]
    pos0 = pl.program_id(2) * QB
    qblk = q_ref[0]
    qp = pos0 + lax.broadcasted_iota(jnp.int32, (1, QB), 1)
    blk = lax.broadcasted_iota(jnp.int32, (NB, 1), 0)

    cmask = ((blk + 1) * CMP_BLOCK - 1) <= qp
    cmf = jnp.where(cmask, 1.0, 0.0)
    ck, cvt = ck_ref[0, 0], cvt_ref[0, 0]
    imp = jnp.zeros((NB, QB), F32)
    o_cmp = []
    for r in range(R):
        s = jnp.where(cmask, _dot_nt(ck, qblk[:, r * Dh:(r + 1) * Dh]), NEG_INF)
        e = jnp.exp(s - jnp.max(s, axis=0, keepdims=True))
        p = e * (1.0 / jnp.sum(e, axis=0, keepdims=True)) * cmf
        o_cmp.append(_dot(cvt, p.astype(BF16)))
        imp = imp + p
    imp = jnp.where(blk == qp // CMP_BLOCK, FORCE_LOCAL,
                    jnp.where(blk == 0, FORCE_INIT, jnp.where(blk * CMP_BLOCK <= qp, imp, -1.0)))
    nsel_sc[...] = jnp.where(_rank_select(imp, N_SEL) > 0.5, 0.0, NEG_INF)

    ones = jnp.ones((ONES_ROWS, KB), BF16)
    qs = [qblk[:, r * Dh:(r + 1) * Dh] for r in range(R)]
    bpt = KB // CMP_BLOCK
    n_t = (pos0 + QB) // KB

    def key_col(kt):
        return kt * KB + lax.broadcasted_iota(jnp.int32, (KB, 1), 0)

    def tile_max(s):
        return jnp.max(s.reshape(KB // 8, 8, QB), axis=0)

    def normalised(acc):
        return acc[:Dh] * (1.0 / acc[Dh:Dh + 1])

    n_all = s_sc.shape[1]
    if n_all % 4 == 0:
        plan = [(4, n_t // 4), (2, (n_t % 4 + 1) // 2)]
    elif n_all % 2 == 0:
        plan = [(2, (n_t + 1) // 2)]
    else:
        plan = [(1, n_t)]

    def score_tiles(it, mx, base, unroll):
        mx = list(mx)
        for u in range(unroll):
            kt = base + it * unroll + u
            k_t = ks_ref[0, 0, pl.ds(pl.multiple_of(kt * KB, KB), KB), :]
            bias = jnp.concatenate([jnp.broadcast_to(nsel_sc[pl.ds(kt * bpt + i, 1), :], (CMP_BLOCK, QB))
                                    for i in range(bpt)], axis=0)
            bias = bias + jnp.where(key_col(kt) <= qp, 0.0, NEG_INF)
            for r in range(R):
                s = _dot_nt(k_t, qs[r]) + bias
                s_sc[r, kt] = s
                mx[r] = jnp.maximum(mx[r], tile_max(s))
        return tuple(mx)

    mx = tuple(jnp.full((8, QB), NEG_INF, F32) for _ in range(R))
    base = 0
    for unroll, trips in plan:
        mx = lax.fori_loop(0, trips, functools.partial(score_tiles, base=base, unroll=unroll), mx)
        base = base + unroll * trips
    m_sel = [jnp.max(x, axis=0, keepdims=True) for x in mx]

    acc_sc[...] = jnp.zeros(acc_sc.shape, F32)

    def value_tiles(it, carry, base, unroll):
        kts = [base + it * unroll + u for u in range(unroll)]
        v1s = [jnp.concatenate([vst_ref[0, 0, kt], ones], axis=0) for kt in kts]
        for r in range(R):
            part = None
            for kt, v1 in zip(kts, v1s):
                pv = _dot(v1, jnp.exp(s_sc[r, kt] - m_sel[r]).astype(BF16))
                part = pv if part is None else part + pv
            acc_sc[r] += part
        return carry

    base = 0
    for unroll, trips in plan:
        lax.fori_loop(0, trips, functools.partial(value_tiles, base=base, unroll=unroll), 0)
        base = base + unroll * trips

    kt_hi = n_t - 1
    w_tiles = []
    for i in range(WINDOW // KB + QB // KB):
        kt = kt_hi - i
        ktc = jnp.maximum(kt, 0)
        key = key_col(kt)
        d = qp - key
        bias = jnp.where(d >= 0, jnp.where(d <= WINDOW, jnp.where(key >= 0, 0.0, NEG_INF), NEG_INF), NEG_INF)
        k_t = kw_ref[0, 0, pl.ds(pl.multiple_of(ktc * KB, KB), KB), :]
        w_tiles.append((k_t, jnp.concatenate([vwt_ref[0, 0, ktc], ones], axis=0), bias))

    gt = jax.nn.sigmoid(ng_ref[0]).T
    ys = []
    for r in range(R):
        s_w = [_dot_nt(k_t, qs[r]) + bias for k_t, _, bias in w_tiles]
        mw = tile_max(s_w[0])
        for s in s_w[1:]:
            mw = jnp.maximum(mw, tile_max(s))
        mw = jnp.max(mw, axis=0, keepdims=True)
        acc_w = _dot(w_tiles[0][1], jnp.exp(s_w[0] - mw).astype(BF16))
        for (_, v1, _), s in zip(w_tiles[1:], s_w[1:]):
            acc_w = acc_w + _dot(v1, jnp.exp(s - mw).astype(BF16))
        ys.append(gt[3 * r:3 * r + 1] * o_cmp[r] + gt[3 * r + 1:3 * r + 2] * normalised(acc_sc[r])
                  + gt[3 * r + 2:3 * r + 3] * normalised(acc_w))
    o_ref[0] = jnp.concatenate(ys, axis=0).T.astype(BF16)


def attn_prompt(q, ck, cvt, ksg, vst, kwg, vwt, z, QB, KB):
    B, S, _ = q.shape
    G, R, Dh = N_KV_HEADS, Q_PER_KV, HEAD_DIM
    NB = ck.shape[2]
    kspec = pl.BlockSpec((1, 1, S, Dh), lambda b, g, c: (b, g, 0, 0))
    vspec = pl.BlockSpec((1, 1, S // KB, Dh, KB), lambda b, g, c: (b, g, 0, 0, 0))
    return pl.pallas_call(
        functools.partial(_attn_prompt_kernel, QB=QB, KB=KB),
        grid=(B, G, S // QB),
        in_specs=[pl.BlockSpec((1, QB, R * Dh), lambda b, g, c: (b, c, g)),
                  pl.BlockSpec((1, 1, NB, Dh), lambda b, g, c: (b, g, 0, 0)),
                  pl.BlockSpec((1, 1, Dh, NB), lambda b, g, c: (b, g, 0, 0)),
                  kspec, vspec, kspec, vspec,
                  pl.BlockSpec((1, QB, LANES), lambda b, g, c: (b, c, COL_NG // LANES + g))],
        out_specs=pl.BlockSpec((1, QB, R * Dh), lambda b, g, c: (b, c, g)),
        out_shape=jax.ShapeDtypeStruct((B, S, G * R * Dh), BF16),
        scratch_shapes=[pltpu.VMEM((NB, QB), F32), pltpu.VMEM((R, S // KB, KB, QB), F32),
                        pltpu.VMEM((R, Dh + ONES_ROWS, QB), F32)],
        compiler_params=_cparams(("parallel", "parallel", "arbitrary")),
        name="attn_prompt",
    )(q, ck, cvt, ksg, vst, kwg, vwt, z)


QPAD = 16


def _pad_rows(x, n):
    return jnp.concatenate([x, jnp.zeros((n - x.shape[0], x.shape[1]), x.dtype)], axis=0)


def _sample_cmp_kernel(q_ref, ck_ref, cv_ref, o_ref, idx_ref, *, NBC):
    G, R = N_KV_HEADS, Q_PER_KV
    for bi in range(q_ref.shape[0]):
        imps = []
        for g in range(G):
            q = _pad_rows(q_ref[bi, g], QPAD).astype(BF16)
            p = _softmax_rows(_dot_nt(q, ck_ref[bi, g]))
            o_ref[bi, g] = _dot(p.astype(BF16), cv_ref[bi, g])[:R]
            imps.append(p[0:1] + p[1:2] + p[2:3] + p[3:4])
        v = jnp.concatenate(imps + [jnp.full((8 - G, NBC), -1.0, F32)], axis=0)
        lane = lax.broadcasted_iota(jnp.int32, v.shape, 1)
        v = jnp.where(lane == 0, -1.0, v)
        out_lane = lax.broadcasted_iota(jnp.int32, (8, LANES), 1)
        idx = jnp.where(out_lane == N_SEL - 1, NBC, 0)
        for t in range(1, N_SEL - 1):
            m = jnp.max(v, axis=-1, keepdims=True)
            first = jnp.min(jnp.where(v == m, lane, NBC), axis=-1, keepdims=True)
            idx = jnp.where(out_lane == t, first, idx)
            v = jnp.where(lane == first, -2.0, v)
        idx_ref[bi] = idx


def sample_cmp(q, ck, cv):
    DB, G, R, Dh = q.shape
    NBC = ck.shape[2]
    bb = 8 if DB % 8 == 0 else 1
    return pl.pallas_call(
        functools.partial(_sample_cmp_kernel, NBC=NBC),
        grid=(DB // bb,),
        in_specs=[pl.BlockSpec((bb, G, R, Dh), lambda b: (b, 0, 0, 0)),
                  pl.BlockSpec((bb, G, NBC, Dh), lambda b: (b, 0, 0, 0)),
                  pl.BlockSpec((bb, G, NBC, Dh), lambda b: (b, 0, 0, 0))],
        out_specs=[pl.BlockSpec((bb, G, R, Dh), lambda b: (b, 0, 0, 0)),
                   pl.BlockSpec((bb, 8, LANES), lambda b: (b, 0, 0))],
        out_shape=[jax.ShapeDtypeStruct((DB, G, R, Dh), F32), jax.ShapeDtypeStruct((DB, 8, LANES), jnp.int32)],
        compiler_params=_cparams(("parallel",)),
        name="sample_cmp",
    )(q, ck, cv)


N_GATHER = N_SEL - 1


def _sample_attn_kernel(idx_ref, pt_ref, *refs):
    del pt_ref
    R, Dh = Q_PER_KV, HEAD_DIM
    blk_refs = refs[:N_GATHER]
    q_ref, ns_ref, win_ref, nw_ref, oc_ref, ng_ref, o_ref = refs[N_GATHER:]
    base = (pl.program_id(0) * pl.num_programs(1) + pl.program_id(1)) * N_SEL
    q = _pad_rows(q_ref[...], QPAD).astype(BF16)
    lane = lax.broadcasted_iota(jnp.int32, (1, LANES), 1)
    row0 = lax.broadcasted_iota(jnp.int32, (LANES, Dh), 0) == 0

    def new_tile(n_ref, kv):
        return jnp.where(row0, jnp.broadcast_to(n_ref[kv], (LANES, Dh)), 0.0).astype(BF16)

    def new_scores(n_ref):
        return jnp.where(lane == 0, _dot_nt(q, new_tile(n_ref, 0)), NEG_INF)

    tiles = []
    for k in range(N_GATHER):
        half = idx_ref[base + k] % 2
        s_k = _dot(q, blk_refs[k][0].astype(BF16))
        tiles.append(jnp.where(lane // CMP_BLOCK == half, s_k, NEG_INF))
    p = _softmax_rows(jnp.concatenate(tiles + [new_scores(ns_ref)], axis=1)).astype(BF16)
    o = _dot(p[:, N_GATHER * LANES:], new_tile(ns_ref, 1))
    for k in range(N_GATHER):
        o = o + _dot_nt(p[:, k * LANES:(k + 1) * LANES], blk_refs[k][1].astype(BF16))
    o_slc = o[:R]

    W = win_ref.shape[2]
    p = _softmax_rows(jnp.concatenate([_dot(q, win_ref[0].astype(BF16)), new_scores(nw_ref)], axis=1)).astype(BF16)
    o_win = (_dot_nt(p[:, :W], win_ref[1].astype(BF16)) + _dot(p[:, W:], new_tile(nw_ref, 1)))[:R]

    gates = jax.nn.sigmoid(ng_ref[...])
    o_cmp = oc_ref[...]
    rows = []
    for r in range(R):
        rows.append(gates[:, 3 * r:3 * r + 1] * o_cmp[r:r + 1] + gates[:, 3 * r + 1:3 * r + 2] * o_slc[r:r + 1]
                    + gates[:, 3 * r + 2:3 * r + 3] * o_win[r:r + 1])
    o_ref[...] = jnp.concatenate(rows, axis=0)


def sample_attn(idx, page_table, layer, slc_t, q, new_s, win_t, new_w, o_cmp, ng):
    DB, G, R, Dh = q.shape
    npg = page_table.shape[1]
    W = win_t.shape[5]
    pt = page_table.reshape(-1)

    def blk_map(k):
        def im(b, g, idx_r, pt_r):
            n = idx_r[(b * G + g) * N_SEL + k]
            return (layer, pt_r[b * npg + n // 2], 0, g, 0, 0)
        return im

    bg = lambda *tail: (lambda b, g, i, p: (b, g) + tail)
    grid_spec = pltpu.PrefetchScalarGridSpec(
        num_scalar_prefetch=2,
        grid=(DB, G),
        in_specs=[pl.BlockSpec((None, None, 2, None, Dh, PAGE), blk_map(k)) for k in range(N_GATHER)]
                 + [pl.BlockSpec((None, None, R, Dh), bg(0, 0)),
                    pl.BlockSpec((None, None, 2, 1, Dh), bg(0, 0, 0)),
                    pl.BlockSpec((None, None, 2, None, Dh, W), lambda b, g, i, p: (layer, b, 0, g, 0, 0)),
                    pl.BlockSpec((None, None, 2, 1, Dh), bg(0, 0, 0)),
                    pl.BlockSpec((None, None, R, Dh), bg(0, 0)),
                    pl.BlockSpec((None, None, 1, LANES), bg(0, 0))],
        out_specs=pl.BlockSpec((None, None, R, Dh), bg(0, 0)),
    )
    return pl.pallas_call(
        _sample_attn_kernel,
        grid_spec=grid_spec,
        out_shape=jax.ShapeDtypeStruct((DB, G, R, Dh), F32),
        compiler_params=_cparams(("arbitrary", "arbitrary")),
        name="sample_attn",
    )(idx, pt, *([slc_t] * N_GATHER), q, new_s, win_t, new_w, o_cmp, ng)


def _merge_kernel(x_ref, ya_ref, yb_ref, yc_ref, m0_ref, m1_ref, m2_ref, wa_ref, wb_ref, wc_ref, wo_ref, o_ref):
    m = (jax.nn.sigmoid(m0_ref[...]) * _dot(ya_ref[...], wa_ref[...])
         + jax.nn.sigmoid(m1_ref[...]) * _dot(yb_ref[...], wb_ref[...])
         + jax.nn.sigmoid(m2_ref[...]) * _dot(yc_ref[...], wc_ref[...]))
    o_ref[...] = x_ref[...] + _dot(m.astype(BF16), wo_ref[...])


def merge_out(x, ya, yb, yc, z, wa, wb, wc, wo, tm):
    T, D = x.shape
    row = pl.BlockSpec((tm, D), lambda i: (i, 0))
    zs = lambda k: pl.BlockSpec((tm, D), lambda i: (i, COL_MG // D + k))
    wsp = pl.BlockSpec((D, D), lambda i: (0, 0))
    return pl.pallas_call(
        _merge_kernel,
        grid=(T // tm,),
        in_specs=[row, row, row, row, zs(0), zs(1), zs(2), wsp, wsp, wsp, wsp],
        out_specs=row,
        out_shape=jax.ShapeDtypeStruct((T, D), F32),
        compiler_params=_cparams(("parallel",)),
        name="merge_out",
    )(x, ya, yb, yc, z, z, z, wa, wb, wc, wo)


def _ffn_kernel(x_ref, g_ref, wg_ref, wu_ref, wd_ref, o_ref, hn_sc, acc_sc):
    j = pl.program_id(1)

    @pl.when(j == 0)
    def _():
        x = x_ref[...]
        ms = jnp.mean(x * x, axis=-1, keepdims=True)
        hn_sc[...] = (x * lax.rsqrt(ms + EPS) * g_ref[...]).astype(BF16)
        acc_sc[...] = jnp.zeros(acc_sc.shape, F32)

    hn = hn_sc[...]
    fg = _dot(hn, wg_ref[...])
    fu = _dot(hn, wu_ref[...])
    act = (fg * jax.nn.sigmoid(fg)) * fu
    acc_sc[...] += _dot(act.astype(BF16), wd_ref[...])

    @pl.when(j == pl.num_programs(1) - 1)
    def _():
        o_ref[...] = x_ref[...] + acc_sc[...]


def ffn(x, gain, w_gu, w_down, tm, fc):
    T, D = x.shape
    F = w_down.shape[0]
    nf = F // fc
    return pl.pallas_call(
        _ffn_kernel,
        grid=(T // tm, nf),
        in_specs=[pl.BlockSpec((tm, D), lambda i, j: (i, 0)),
                  pl.BlockSpec((1, D), lambda i, j: (0, 0)),
                  pl.BlockSpec((D, fc), lambda i, j: (0, j)),
                  pl.BlockSpec((D, fc), lambda i, j: (0, nf + j)),
                  pl.BlockSpec((fc, D), lambda i, j: (j, 0))],
        out_specs=pl.BlockSpec((tm, D), lambda i, j: (i, 0)),
        out_shape=jax.ShapeDtypeStruct((T, D), F32),
        scratch_shapes=[pltpu.VMEM((tm, D), BF16), pltpu.VMEM((tm, D), F32)],
        compiler_params=_cparams(("parallel", "arbitrary")),
        name="ffn",
    )(x, gain.reshape(1, D), w_gu, w_gu, w_down)


def _rmsnorm_kernel(x_ref, g_ref, o_ref):
    x = x_ref[...]
    ms = jnp.mean(x * x, axis=-1, keepdims=True)
    o_ref[...] = x * lax.rsqrt(ms + EPS) * g_ref[...]


def rmsnorm_rows(x, gain, tm):
    T, D = x.shape
    return pl.pallas_call(
        _rmsnorm_kernel,
        grid=(T // tm,),
        in_specs=[pl.BlockSpec((tm, D), lambda i: (i, 0)), pl.BlockSpec((1, D), lambda i: (0, 0))],
        out_specs=pl.BlockSpec((tm, D), lambda i: (i, 0)),
        out_shape=jax.ShapeDtypeStruct((T, D), F32),
        compiler_params=_cparams(("parallel",)),
        name="final_rmsnorm",
    )(x, gain.reshape(1, D))


def _prep_w_in(w):
    D = w.shape[0]
    G, R = N_KV_HEADS, Q_PER_KV
    n_gate = 3 * G * R
    head, gates, tail = w[:, :COL_NG], w[:, COL_NG:COL_NG + n_gate], w[:, COL_NG + n_gate:]
    gates = gates.reshape(D, G, 3 * R)
    gates = jnp.pad(gates, ((0, 0), (0, 0), (0, LANES - 3 * R))).reshape(D, G * LANES)
    return jnp.concatenate([head, gates, tail], axis=1).astype(BF16)


RG_TILE = 256


def _prep_blockdiag(w):
    n, c, _ = w.shape
    per = RG_TILE // c
    w = w.reshape(n // per, per, c, c)
    eye = jnp.eye(per, dtype=w.dtype)
    t = jnp.einsum('mpcd,pq->mpcqd', w, eye)
    return t.reshape(n // per, RG_TILE, RG_TILE).astype(BF16)


def _tile(n, pref):
    t = min(n, pref)
    assert n % t == 0, (n, t)
    return t


def _token_minor(cache):
    nd = cache.ndim
    return jnp.transpose(cache, tuple(range(nd - 4)) + (nd - 3, nd - 2, nd - 1, nd - 4))


def kernel(x_prompt, x_sample, cache_cmp_kv, cache_slc_kv, cache_win_kv, state_rglru_conv, state_rglru_h,
           state_conf_conv, page_table, norm1_g, w_in, rg_conv_w, rg_conv_b, rg_wa, rg_ba, rg_wi, rg_bi,
           rg_lambda, cf_dw_w, cf_dw_b, cf_ln_g, cf_ln_b, nsa_w_ck, nsa_w_cv, w_br_rnn, w_br_conv, w_br_attn,
           w_out, norm2_g, w_ffn_gu, w_ffn_down, final_norm_g):
    B, S, D = x_prompt.shape
    DB = x_sample.shape[0]
    depth = w_in.shape[0]
    G, R, Dh = N_KV_HEADS, Q_PER_KV, HEAD_DIM
    n_pages = page_table.shape[1]
    past_len = n_pages * PAGE
    w_buf = cache_win_kv.shape[2]
    d_ff = w_ffn_down.shape[1]
    rg_k = rg_conv_w.shape[1]
    cf_k = cf_dw_w.shape[1]
    assert D == D_MODEL and x_sample.shape[1] == 1 and cache_cmp_kv.shape[2] == PAGE
    assert cache_cmp_kv.shape[3:] == (2, G, Dh) and w_buf == min(WINDOW, past_len)
    assert S % PAGE == 0 and cf_k - 1 <= CF_PAD

    T = B * S
    tm_p = _tile(T, 1024)
    tm_rope = _tile(S, 512)
    tt = _tile(S, 256)
    QB = _tile(S, 256)
    KB = _tile(S, 256)
    assert QB % KB == 0 and WINDOW % KB == 0 and KB % CMP_BLOCK == 0
    fc = d_ff // 2 if (d_ff // 2) % LANES == 0 else d_ff

    cos_p, sin_p = rope_tables(jnp.arange(S, dtype=jnp.int32))
    cos_s, sin_s = rope_tables(jnp.full((DB,), past_len, jnp.int32))
    prompt_pages = jnp.arange(T // PAGE, dtype=jnp.int32)
    sample_pages = page_table.reshape(-1).astype(jnp.int32)
    cmp_t, slc_t, win_t = _token_minor(cache_cmp_kv), _token_minor(cache_slc_kv), _token_minor(cache_win_kv)
    row2 = lambda v: v.reshape(1, -1)

    xp = x_prompt.reshape(T, D)
    xs = x_sample.reshape(DB, D)
    st_p, st_s = [], []
    for l in range(depth):
        w_in_l = _prep_w_in(w_in[l])
        wa_l, wi_l = _prep_blockdiag(rg_wa[l]), _prep_blockdiag(rg_wi[l])
        wk_c, wv_c = _prep_cmp_weight(nsa_w_ck[l]), _prep_cmp_weight(nsa_w_cv[l])
        w_a, w_b, w_c = w_br_rnn[l].astype(BF16), w_br_conv[l].astype(BF16), w_br_attn[l].astype(BF16)
        w_o, w_gu, w_dn = w_out[l].astype(BF16), w_ffn_gu[l].astype(BF16), w_ffn_down[l].astype(BF16)
        rg_args = (rg_conv_w[l], row2(rg_conv_b[l]), wa_l, row2(rg_ba[l]), wi_l, row2(rg_bi[l]),
                   row2(rg_lambda[l]))
        cf_args = (cf_dw_w[l], row2(cf_dw_b[l]), row2(cf_ln_g[l]), row2(cf_ln_b[l]))

        z = norm_matmul(xp, norm1_g[l], w_in_l, tm_p, 1024).reshape(B, S, N_PAD)
        ya, h_last = rglru_prompt(z, *rg_args, tt)
        yb, cf_tail = conformer_prompt(z, *cf_args, tt)
        q, kvc, kvs, kvw, kvc_t, ksg, vst, kwg, vwt = rope_split(z, cos_p, sin_p, tm_rope, KB)
        ck, cv = compress_pages(kvc_t.reshape(1, T // PAGE, 2, G, Dh, PAGE), 0, prompt_pages, wk_c, wv_c)
        ck = _block_major(ck, B, S // PAGE).astype(BF16)
        cvt = jnp.transpose(_block_major(cv, B, S // PAGE), (0, 1, 3, 2)).astype(BF16)
        yc = attn_prompt(q, ck, cvt, ksg, vst, kwg, vwt, z, QB, KB)
        xp = merge_out(xp, ya.reshape(T, D), yb.reshape(T, D), yc.reshape(T, D), z.reshape(T, N_PAD),
                       w_a, w_b, w_c, w_o, _tile(T, 512))
        xp = ffn(xp, norm2_g[l], w_gu, w_dn, _tile(T, 512), fc)
        kv6 = lambda a: a.reshape(B, S, 2, G, Dh)
        st_p.append((kv6(kvc), kv6(kvs), kv6(kvw)[:, S - w_buf:] if S >= w_buf else
                     jnp.pad(kv6(kvw), ((0, 0), (w_buf - S, 0), (0, 0), (0, 0), (0, 0))),
                     z[:, S - (rg_k - 1):, COL_ZR:COL_ZR + D], h_last.reshape(B, D),
                     cf_tail[:, CF_PAD - (cf_k - 1):]))

        zs = norm_matmul(xs, norm1_g[l], w_in_l, DB, 1024)
        rgst = jnp.transpose(state_rglru_conv[l], (1, 0, 2))
        cfst = jnp.transpose(state_conf_conv[l], (1, 0, 2))
        ya_s, yb_s, h_s, u_s = sample_mix(zs, rgst, state_rglru_h[l], cfst, *rg_args, *cf_args)
        q_s, kvc_s, kvs_s, kvw_s = rope_split(zs.reshape(1, DB, N_PAD), cos_s, sin_s, DB)
        q_s = q_s.reshape(DB, G, R, Dh).astype(F32)
        ck_s, cv_s = compress_pages(cmp_t, l, sample_pages, wk_c, wv_c)
        o_cmp_s, idx = sample_cmp(q_s, _block_major(ck_s, DB, n_pages).astype(BF16),
                                  _block_major(cv_s, DB, n_pages).astype(BF16))
        idx = idx[:, :G, :N_SEL].reshape(-1)
        new_rows = lambda a: jnp.transpose(a.reshape(DB, 2, G, 1, Dh), (0, 2, 1, 3, 4))
        ng_s = zs[:, COL_NG:COL_NG + G * LANES].reshape(DB, G, 1, LANES)
        yc_s = sample_attn(idx, page_table.astype(jnp.int32), l, slc_t, q_s, new_rows(kvs_s[0]), win_t,
                           new_rows(kvw_s[0]), o_cmp_s, ng_s)
        xs = merge_out(xs, ya_s, yb_s, yc_s.reshape(DB, D).astype(BF16), zs, w_a, w_b, w_c, w_o, DB)
        xs = ffn(xs, norm2_g[l], w_gu, w_dn, DB, fc)
        kv6s = lambda a: a.reshape(DB, 1, 2, G, Dh)
        st_s.append((kv6s(kvc_s), kv6s(kvs_s),
                     jnp.concatenate([cache_win_kv[l], kv6s(kvw_s)], axis=1)[:, 1:],
                     jnp.concatenate([state_rglru_conv[l], zs[:, None, COL_ZR:COL_ZR + D]], axis=1)[:, 1:],
                     h_s,
                     jnp.concatenate([state_conf_conv[l], u_s[:, None]], axis=1)[:, 1:]))

    y_prompt = rmsnorm_rows(xp, final_norm_g, tm_p).reshape(B, S, D)
    y_sample = rmsnorm_rows(xs, final_norm_g, DB).reshape(DB, 1, D)
    stk = lambda sts, j: jnp.stack([st[j] for st in sts], axis=0)
    return (y_prompt, y_sample,
            stk(st_p, 0), stk(st_p, 1), stk(st_p, 2), stk(st_p, 3), stk(st_p, 4), stk(st_p, 5),
            stk(st_s, 0), stk(st_s, 1), stk(st_s, 2), stk(st_s, 3), stk(st_s, 4), stk(st_s, 5))
```
